```python
import jax, jax.numpy as jnp
from jax import lax
import numpy as np

D_MODEL = 1024
BATCH = 2
SEQ = 8192
DEPTH = 2

N_A = DEPTH // 2
N_B = DEPTH - N_A
N_DENSE = (DEPTH + 1) // 2
N_MOE = DEPTH // 2

HEAD_DIM = 128
N_HEADS_A = D_MODEL // HEAD_DIM
N_KV_HEADS_A = 2
GROUP_A = N_HEADS_A // N_KV_HEADS_A
N_IDX_HEADS = 8
IDX_DIM = 64
MAX_TOPK = 256
N_HEADS_B = D_MODEL // HEAD_DIM

ROPE_THETA = 500000.0
ROT_FRAC = 4
Q_BLOCK = 128
NORM_EPS = 1e-6

D_FF = 2816
N_EXPERTS = 8
TOP_K_EXPERTS = 2

A_Q = N_HEADS_A * HEAD_DIM
A_K = N_KV_HEADS_A * HEAD_DIM
A_V = N_KV_HEADS_A * HEAD_DIM
A_QI = N_IDX_HEADS * IDX_DIM
A_KI = IDX_DIM
A_WI = N_IDX_HEADS
A_IN = A_Q + A_K + A_V + A_QI + A_KI + A_WI
A_SPLITS = (A_Q, A_Q + A_K, A_Q + A_K + A_V, A_Q + A_K + A_V + A_QI, A_Q + A_K + A_V + A_QI + A_KI)
KV_B = 2 * N_HEADS_B * HEAD_DIM

kernel_name = "yoco_dsa_stickbreaking_moe_block"


def rms_norm(x, w):
    xf = x.astype(jnp.float32)
    y = xf * lax.rsqrt(jnp.mean(xf * xf, axis=-1, keepdims=True) + NORM_EPS)
    return (y * w.astype(jnp.float32)).astype(x.dtype)


def partial_rope(x, pos):
    d = x.shape[-1]
    r = d // ROT_FRAC
    half = r // 2
    inv = ROPE_THETA ** (-jnp.arange(half, dtype=jnp.float32) * 2.0 / r)
    ang = pos.astype(jnp.float32)[:, None] * inv[None, :]
    shape = (1, pos.shape[0]) + (1,) * (x.ndim - 3) + (half,)
    cos = jnp.cos(ang).reshape(shape).astype(x.dtype)
    sin = jnp.sin(ang).reshape(shape).astype(x.dtype)
    x1, x2, rest = x[..., :half], x[..., half:r], x[..., r:]
    return jnp.concatenate([x1 * cos - x2 * sin, x2 * cos + x1 * sin, rest], axis=-1)


def swiglu(u, w_gate_up, w_down):
    g, up = jnp.split(u @ w_gate_up, 2, axis=-1)
    return (jax.nn.silu(g) * up) @ w_down


def dsa_attention(h, norm_w, w_in, q_norm_w, k_norm_w, w_out):
    B, S, _ = h.shape
    topk = min(MAX_TOPK, S // 4)
    pos = jnp.arange(S)
    u = rms_norm(h, norm_w)
    q, k, v, qi, ki, wi = jnp.split(u @ w_in, A_SPLITS, axis=-1)
    q = q.reshape(B, S, N_KV_HEADS_A, GROUP_A, HEAD_DIM)
    k = k.reshape(B, S, N_KV_HEADS_A, HEAD_DIM)
    v = v.reshape(B, S, N_KV_HEADS_A, HEAD_DIM)
    q = partial_rope(rms_norm(q, q_norm_w), pos)
    k = partial_rope(rms_norm(k, k_norm_w), pos)
    qi = partial_rope(qi.reshape(B, S, N_IDX_HEADS, IDX_DIM), pos)
    ki = partial_rope(ki[:, :, None, :], pos)[:, :, 0, :]
    wi = wi * (N_IDX_HEADS ** -0.5)
    s_idx = jnp.arange(S)
    scale = HEAD_DIM ** -0.5
    gather = jax.vmap(lambda kb, ib: kb[ib])

    def block(i):
        start = i * Q_BLOCK
        t = start + jnp.arange(Q_BLOCK)
        qb = lax.dynamic_slice_in_dim(q, start, Q_BLOCK, axis=1)
        qib = lax.dynamic_slice_in_dim(qi, start, Q_BLOCK, axis=1)
        wib = lax.dynamic_slice_in_dim(wi, start, Q_BLOCK, axis=1)
        causal = s_idx[None, :] <= t[:, None]
        rel = jax.nn.relu(jnp.einsum('bqhd,bsd->bqhs', qib, ki).astype(jnp.float32) * (IDX_DIM ** -0.5))
        score = jnp.einsum('bqhs,bqh->bqs', rel, wib.astype(jnp.float32))
        score = jnp.where(causal[None], score, -jnp.inf)
        _, sel = lax.top_k(score, topk)
        sel_valid = sel <= t[None, :, None]
        k_sel = gather(k, sel)
        v_sel = gather(v, sel)
        logits = jnp.einsum('bqhgd,bqkhd->bqhgk', qb, k_sel).astype(jnp.float32) * scale
        logits = jnp.where(sel_valid[:, :, None, None, :], logits, -jnp.inf)
        p = jax.nn.softmax(logits, axis=-1).astype(v.dtype)
        o = jnp.einsum('bqhgk,bqkhd->bqhgd', p, v_sel)
        return o.reshape(B, Q_BLOCK, A_Q)

    o = lax.map(block, jnp.arange(S // Q_BLOCK))
    o = jnp.transpose(o, (1, 0, 2, 3)).reshape(B, S, A_Q)
    return o @ w_out


def stick_breaking_attention(h, norm_w, w_q, w_out, k_sb, v_sb):
    B, S, _ = h.shape
    u = rms_norm(h, norm_w)
    q = (u @ w_q).reshape(B, S, N_HEADS_B, HEAD_DIM)
    s_idx = jnp.arange(S)
    scale = HEAD_DIM ** -0.5

    def block(i):
        start = i * Q_BLOCK
        t = start + jnp.arange(Q_BLOCK)
        qb = lax.dynamic_slice_in_dim(q, start, Q_BLOCK, axis=1)
        strict = (s_idx[None, :] < t[:, None])[None, None]
        z = jnp.einsum('bqhd,bshd->bhqs', qb, k_sb).astype(jnp.float32) * scale
        log_beta = jax.nn.log_sigmoid(z)
        log_keep = jnp.where(strict, jax.nn.log_sigmoid(-z), 0.0)
        after = lax.cumsum(log_keep, axis=3, reverse=True) - log_keep
        a = jnp.where(strict, jnp.exp(log_beta + after), 0.0)
        o = jnp.einsum('bhqs,bshd->bqhd', a.astype(v_sb.dtype), v_sb)
        return o.reshape(B, Q_BLOCK, N_HEADS_B * HEAD_DIM)

    o = lax.map(block, jnp.arange(S // Q_BLOCK))
    o = jnp.transpose(o, (1, 0, 2, 3)).reshape(B, S, N_HEADS_B * HEAD_DIM)
    return o @ w_out


def moe_ffn(u, router_w, w_gate_up, w_down):
    B, S, D = u.shape
    tok = u.reshape(B * S, D)
    logits = (tok @ router_w).astype(jnp.float32)
    top_vals, top_idx = lax.top_k(logits, TOP_K_EXPERTS)
    gates = jax.nn.softmax(top_vals, axis=-1)
    dense_gates = jnp.sum(jax.nn.one_hot(top_idx, N_EXPERTS, dtype=jnp.float32) * gates[..., None], axis=1)
    y = jnp.zeros_like(tok)
    for e in range(N_EXPERTS):
        y = y + dense_gates[:, e:e + 1].astype(tok.dtype) * swiglu(tok, w_gate_up[e], w_down[e])
    return y.reshape(B, S, D)


def setup_inputs(seed: int = 0) -> dict:
    key = jax.random.key(seed)
    ks = iter(jax.random.split(key, 32))

    def w(shape, fan_in):
        return jax.random.normal(next(ks), shape, jnp.float32) * (fan_in ** -0.5)

    def gain(shape):
        return 1.0 + 0.02 * jax.random.normal(next(ks), shape, jnp.float32)

    return {
        "x": jax.random.normal(next(ks), (BATCH, SEQ, D_MODEL), jnp.float32),
        "a_norm_w": gain((N_A, D_MODEL)),
        "a_w_in": w((N_A, D_MODEL, A_IN), D_MODEL),
        "a_q_norm_w": gain((N_A, HEAD_DIM)),
        "a_k_norm_w": gain((N_A, HEAD_DIM)),
        "a_w_out": w((N_A, A_Q, D_MODEL), A_Q),
        "kv_norm_w": gain((D_MODEL,)),
        "kv_w": w((D_MODEL, KV_B), D_MODEL),
        "b_norm_w": gain((N_B, D_MODEL)),
        "b_w_q": w((N_B, D_MODEL, N_HEADS_B * HEAD_DIM), D_MODEL),
        "b_w_out": w((N_B, N_HEADS_B * HEAD_DIM, D_MODEL), N_HEADS_B * HEAD_DIM),
        "ffn_norm_w": gain((DEPTH, D_MODEL)),
        "dense_w_gate_up": w((N_DENSE, D_MODEL, 2 * D_FF), D_MODEL),
        "dense_w_down": w((N_DENSE, D_FF, D_MODEL), D_FF),
        "moe_router_w": w((N_MOE, D_MODEL, N_EXPERTS), D_MODEL),
        "moe_w_gate_up": w((N_MOE, N_EXPERTS, D_MODEL, 2 * D_FF), D_MODEL),
        "moe_w_down": w((N_MOE, N_EXPERTS, D_FF, D_MODEL), D_FF),
    }


def reference(x, a_norm_w, a_w_in, a_q_norm_w, a_k_norm_w, a_w_out, kv_norm_w, kv_w,
              b_norm_w, b_w_q, b_w_out, ffn_norm_w, dense_w_gate_up, dense_w_down,
              moe_router_w, moe_w_gate_up, moe_w_down):
    B, S, _ = x.shape
    h = x
    k_sb = v_sb = None
    for l in range(DEPTH):
        if l < N_A:
            h = h + dsa_attention(h, a_norm_w[l], a_w_in[l], a_q_norm_w[l], a_k_norm_w[l], a_w_out[l])
        else:
            j = l - N_A
            h = h + stick_breaking_attention(h, b_norm_w[j], b_w_q[j], b_w_out[j], k_sb, v_sb)
        u = rms_norm(h, ffn_norm_w[l])
        if l % 2 == 0:
            h = h + swiglu(u, dense_w_gate_up[l // 2], dense_w_down[l // 2])
        else:
            m = l // 2
            h = h + moe_ffn(u, moe_router_w[m], moe_w_gate_up[m], moe_w_down[m])
        if l == N_A - 1:
            kv = rms_norm(h, kv_norm_w) @ kv_w
            k_sb, v_sb = jnp.split(kv, 2, axis=-1)
            k_sb = k_sb.reshape(B, S, N_HEADS_B, HEAD_DIM)
            v_sb = v_sb.reshape(B, S, N_HEADS_B, HEAD_DIM)
    return h
```

```python
import functools

import jax
import jax.numpy as jnp
from jax import lax
from jax.experimental import pallas as pl
from jax.experimental.pallas import tpu as pltpu

D_MODEL = 1024
HEAD_DIM = 128
N_HEADS_A = 8
N_KV_HEADS_A = 2
GROUP_A = 4
N_IDX_HEADS = 8
IDX_DIM = 64
MAX_TOPK = 256
N_HEADS_B = 8
ROPE_THETA = 500000.0
ROT_FRAC = 4
NORM_EPS = 1e-6
D_FF = 2816
N_EXPERTS = 8

A_Q = N_HEADS_A * HEAD_DIM
A_K = N_KV_HEADS_A * HEAD_DIM
A_V = N_KV_HEADS_A * HEAD_DIM
A_QI = N_IDX_HEADS * IDX_DIM
A_IN = A_Q + A_K + A_V + A_QI + IDX_DIM + N_IDX_HEADS
A_IN_PAD = A_Q + A_K + A_V + A_QI + 128

LANE = 128
INT_MIN = -(2 ** 31)
VMEM_LIMIT = 56 * 1024 * 1024

F32 = jnp.float32
BF16 = jnp.bfloat16
NT_DIMS = (((1,), (1,)), ((), ()))


def _params(sem):
    return pltpu.CompilerParams(dimension_semantics=sem, vmem_limit_bytes=VMEM_LIMIT)


def _rms(x, w):
    ms = jnp.mean(x * x, axis=-1, keepdims=True)
    return x * lax.rsqrt(ms + NORM_EPS) * w


def _norm_matmul_kernel(x_ref, nw_ref, w_ref, o_ref):
    u = _rms(x_ref[...], nw_ref[...]).astype(BF16)
    o_ref[...] = jnp.dot(u, w_ref[...], preferred_element_type=F32).astype(o_ref.dtype)


def _norm_matmul(x, nw, w, out_dtype, bm=512):
    n, d = x.shape
    m = w.shape[1]
    return pl.pallas_call(
        _norm_matmul_kernel,
        grid=(n // bm,),
        in_specs=[pl.BlockSpec((bm, d), lambda i: (i, 0)),
                  pl.BlockSpec((1, d), lambda i: (0, 0)),
                  pl.BlockSpec((d, m), lambda i: (0, 0))],
        out_specs=pl.BlockSpec((bm, m), lambda i: (i, 0)),
        out_shape=jax.ShapeDtypeStruct((n, m), out_dtype),
        compiler_params=_params(("parallel",)),
        name="norm_matmul",
    )(x, nw.reshape(1, d), w)


def _matmul_res_kernel(a_ref, w_ref, r_ref, o_ref):
    o_ref[...] = r_ref[...] + jnp.dot(a_ref[...], w_ref[...], preferred_element_type=F32)


def _matmul_residual(a, w, res, bm=512):
    n, k = a.shape
    m = w.shape[1]
    return pl.pallas_call(
        _matmul_res_kernel,
        grid=(n // bm,),
        in_specs=[pl.BlockSpec((bm, k), lambda i: (i, 0)),
                  pl.BlockSpec((k, m), lambda i: (0, 0)),
                  pl.BlockSpec((bm, m), lambda i: (i, 0))],
        out_specs=pl.BlockSpec((bm, m), lambda i: (i, 0)),
        out_shape=jax.ShapeDtypeStruct((n, m), F32),
        compiler_params=_params(("parallel",)),
        name="matmul_residual",
    )(a, w, res)


def _rope(x, c, s_lo, s_hi, half):
    n = x.shape[-1]
    return x * c + pltpu.roll(x, n - half, 1) * s_lo + pltpu.roll(x, half, 1) * s_hi


def _dsa_prep_kernel(p_ref, qw_ref, kw_ref, c_ref, slo_ref, shi_ref, ci_ref, sloi_ref, shii_ref,
                     q_ref, k_ref, vt_ref, qit_ref, ki_ref, wt_ref, *, tq):
    bm = p_ref.shape[0]
    c, slo, shi = c_ref[...], slo_ref[...], shi_ref[...]
    ci, sloi, shii = ci_ref[...], sloi_ref[...], shii_ref[...]
    scale = HEAD_DIM ** -0.5
    for h in range(N_HEADS_A):
        xh = _rms(p_ref[:, h * HEAD_DIM:(h + 1) * HEAD_DIM], qw_ref[...])
        q_ref[:, h * HEAD_DIM:(h + 1) * HEAD_DIM] = (_rope(xh, c, slo, shi, 16) * scale).astype(BF16)
    for h in range(N_KV_HEADS_A):
        xh = _rms(p_ref[:, A_Q + h * HEAD_DIM:A_Q + (h + 1) * HEAD_DIM], kw_ref[...])
        k_ref[:, h * HEAD_DIM:(h + 1) * HEAD_DIM] = _rope(xh, c, slo, shi, 16).astype(BF16)
    v = p_ref[:, A_Q + A_K:A_Q + A_K + A_V]
    vt_ref[0, 0] = v.T.astype(BF16)
    off = A_Q + A_K + A_V
    for j in range(A_QI // LANE):
        xj = _rope(p_ref[:, off + j * LANE:off + (j + 1) * LANE], ci, sloi, shii, 8)
        xt = xj.T.astype(BF16)
        for b in range(bm // tq):
            qit_ref[0, b, j * LANE:(j + 1) * LANE, :] = xt[:, b * tq:(b + 1) * tq]
    off += A_QI
    lane = lax.broadcasted_iota(jnp.int32, (1, LANE), 1)
    first = lane < IDX_DIM
    kw = p_ref[:, off:off + LANE]
    kw = _rope(kw, jnp.where(first, ci, 1.0), jnp.where(first, sloi, 0.0), jnp.where(first, shii, 0.0), 8)
    ki_ref[...] = kw[:, :IDX_DIM].astype(BF16)
    wt = kw.T[IDX_DIM:IDX_DIM + N_IDX_HEADS, :] * (N_IDX_HEADS ** -0.5) * (IDX_DIM ** -0.5)
    for b in range(bm // tq):
        wt_ref[0, b] = wt[:, b * tq:(b + 1) * tq]


def _rope_tables(s, head, reps):
    r = head // ROT_FRAC
    half = r // 2
    inv = ROPE_THETA ** (-jnp.arange(half, dtype=F32) * 2.0 / r)
    ang = jnp.arange(s).astype(F32)[:, None] * inv[None, :]
    cos, sin = jnp.cos(ang), jnp.sin(ang)
    one = jnp.ones((s, head - r), F32)
    zero = jnp.zeros((s, head - r), F32)
    zh = jnp.zeros((s, half), F32)
    c = jnp.concatenate([cos, cos, one], axis=1)
    s_lo = jnp.concatenate([-sin, zh, zero], axis=1)
    s_hi = jnp.concatenate([zh, sin, zero], axis=1)
    return tuple(jnp.tile(t, (1, reps)) for t in (c, s_lo, s_hi))


def _dsa_prep(proj, qw, kw, b, s, tq, ch):
    n = proj.shape[0]
    bm = ch
    nb = s // bm
    c, slo, shi = _rope_tables(s, HEAD_DIM, 1)
    ci, sloi, shii = _rope_tables(s, IDX_DIM, 2)
    tab = pl.BlockSpec((bm, LANE), lambda i: (i % nb, 0))
    row = lambda w: pl.BlockSpec((bm, w), lambda i: (i, 0))
    out_shapes = (
        jax.ShapeDtypeStruct((n, A_Q), BF16),
        jax.ShapeDtypeStruct((n, A_K), BF16),
        jax.ShapeDtypeStruct((b, s // ch, A_V, ch), BF16),
        jax.ShapeDtypeStruct((b, s // tq, A_QI, tq), BF16),
        jax.ShapeDtypeStruct((n, IDX_DIM), BF16),
        jax.ShapeDtypeStruct((b, s // tq, N_IDX_HEADS, tq), F32),
    )
    out_specs = (
        row(A_Q), row(A_K),
        pl.BlockSpec((1, 1, A_V, ch), lambda i: (i // nb, i % nb, 0, 0)),
        pl.BlockSpec((1, bm // tq, A_QI, tq), lambda i: (i // nb, i % nb, 0, 0)),
        row(IDX_DIM),
        pl.BlockSpec((1, bm // tq, N_IDX_HEADS, tq), lambda i: (i // nb, i % nb, 0, 0)),
    )
    return pl.pallas_call(
        functools.partial(_dsa_prep_kernel, tq=tq),
        grid=(n // bm,),
        in_specs=[row(A_IN_PAD),
                  pl.BlockSpec((1, HEAD_DIM), lambda i: (0, 0)),
                  pl.BlockSpec((1, HEAD_DIM), lambda i: (0, 0)),
                  tab, tab, tab, tab, tab, tab],
        out_specs=out_specs,
        out_shape=out_shapes,
        compiler_params=_params(("parallel",)),
        name="dsa_prep",
    )(proj, qw.reshape(1, -1), kw.reshape(1, -1), c, slo, shi, ci, sloi, shii)


def _dsa_attn_kernel(qit_ref, wt_ref, ki_ref, q_ref, k_ref, vt_ref, o_ref,
                     keys_ref, thr_ref, cnt_ref, acc_ref, *, tq, ch, topk):
    i = pl.program_id(1)
    nch = ((i + 1) * tq + ch - 1) // ch
    t_row = i * tq + lax.broadcasted_iota(jnp.int32, (1, tq), 1)

    qit = jnp.concatenate([qit_ref[0, 0, h * IDX_DIM:(h + 1) * IDX_DIM, :] for h in range(N_IDX_HEADS)],
                          axis=1)
    wt = wt_ref[0, 0]

    def score_body(c, carry):
        s0 = pl.multiple_of(c * ch, ch)
        rel = jnp.dot(ki_ref[pl.ds(s0, ch), :], qit, preferred_element_type=F32)
        sc = jnp.zeros((ch, tq), F32)
        for h in range(N_IDX_HEADS):
            sc = sc + jnp.maximum(rel[:, h * tq:(h + 1) * tq], 0.0) * wt[h:h + 1, :]
        sc = jnp.where(sc == 0.0, 0.0, sc)
        bits = lax.bitcast_convert_type(sc, jnp.int32)
        key = bits ^ ((bits >> 31) & 0x7FFFFFFF)
        s_idx = s0 + lax.broadcasted_iota(jnp.int32, (ch, tq), 0)
        keys_ref[pl.ds(s0, ch), :] = jnp.where(s_idx <= t_row, key, INT_MIN)
        return carry

    lax.fori_loop(0, nch, score_body, 0)

    def count_ge(thr):
        def body(c, acc):
            s0 = pl.multiple_of(c * ch, ch)
            hit = jnp.where(keys_ref[pl.ds(s0, ch), :] >= thr, 1, 0)
            return acc + jnp.sum(hit.reshape(ch // 8, 8, tq), axis=0)
        acc = lax.fori_loop(0, nch, body, jnp.zeros((8, tq), jnp.int32))
        return jnp.sum(acc, axis=0, keepdims=True)

    short = t_row < topk
    c0 = count_ge(jnp.zeros((1, tq), jnp.int32))
    pos = c0 >= topk
    thr_ref[...] = jnp.where(pos, 0, INT_MIN)
    cnt_ref[...] = jnp.where(pos, c0, -1)

    def open_rows():
        return jnp.logical_and(jnp.logical_not(short), cnt_ref[...] != topk)

    def n_open():
        return jnp.sum(jnp.where(open_rows(), 1, 0))

    def bis_cond(st):
        bit, n = st
        return jnp.logical_and(bit >= 0, n > 0)

    def bis_body(st):
        bit, _ = st
        cand = thr_ref[...] + (jnp.int32(1) << bit)
        c = count_ge(cand)
        take = jnp.logical_and(c >= topk, open_rows())
        thr_ref[...] = jnp.where(take, cand, thr_ref[...])
        cnt_ref[...] = jnp.where(take, c, cnt_ref[...])
        return bit - 1, n_open()

    _, left = lax.while_loop(bis_cond, bis_body, (jnp.int32(30), n_open()))

    @pl.when(left > 0)
    def _():
        thr = jnp.where(open_rows(), thr_ref[...], INT_MIN)

        def gt_body(c, acc):
            s0 = pl.multiple_of(c * ch, ch)
            hit = jnp.where(keys_ref[pl.ds(s0, ch), :] > thr, 1, 0)
            return acc + jnp.sum(hit.reshape(ch // 8, 8, tq), axis=0)
        gt = jnp.sum(lax.fori_loop(0, nch, gt_body, jnp.zeros((8, tq), jnp.int32)), axis=0, keepdims=True)
        need = (topk - gt).astype(F32)
        r = lax.broadcasted_iota(jnp.int32, (ch, ch), 0)
        cc = lax.broadcasted_iota(jnp.int32, (ch, ch), 1)
        tri = jnp.where(cc <= r, 1.0, 0.0).astype(BF16)

        def tie_body(c, run):
            s0 = pl.multiple_of(c * ch, ch)
            kc = keys_ref[pl.ds(s0, ch), :]
            eq = kc == thr
            eqf = jnp.where(eq, 1.0, 0.0)
            pre = jnp.dot(tri, eqf.astype(BF16), preferred_element_type=F32) + run
            drop = jnp.logical_and(eq, pre > need)
            keys_ref[pl.ds(s0, ch), :] = jnp.where(drop, INT_MIN, kc)
            return run + jnp.sum(eqf, axis=0, keepdims=True)

        lax.fori_loop(0, nch, tie_body, jnp.zeros((1, tq), F32))

    thr = jnp.where(short, INT_MIN + 1, thr_ref[...])

    for g in range(N_KV_HEADS_A):
        qg = jnp.concatenate([q_ref[:, (g * GROUP_A + j) * HEAD_DIM:(g * GROUP_A + j + 1) * HEAD_DIM]
                              for j in range(GROUP_A)], axis=0)
        acc_ref[...] = jnp.zeros(acc_ref.shape, F32)

        def att_body(c, carry):
            m, l = carry
            s0 = pl.multiple_of(c * ch, ch)
            kc = k_ref[pl.ds(s0, ch), g * HEAD_DIM:(g + 1) * HEAD_DIM]
            lg = lax.dot_general(kc, qg, NT_DIMS, preferred_element_type=F32)
            bias = jnp.where(keys_ref[pl.ds(s0, ch), :] >= thr, 0.0, -jnp.inf)
            lg = lg + jnp.concatenate([bias] * GROUP_A, axis=1)
            m_new = jnp.maximum(m, jnp.max(lg, axis=0, keepdims=True))
            m_safe = jnp.where(m_new == -jnp.inf, 0.0, m_new)
            p = jnp.exp(lg - m_safe)
            alpha = jnp.exp(m - m_safe)
            l = alpha * l + jnp.sum(p, axis=0, keepdims=True)
            pv = jnp.dot(vt_ref[0, c, g * HEAD_DIM:(g + 1) * HEAD_DIM, :], p.astype(BF16),
                         preferred_element_type=F32)
            acc_ref[...] = acc_ref[...] * alpha + pv
            return m_new, l

        m0 = jnp.full((1, GROUP_A * tq), -jnp.inf, F32)
        l0 = jnp.zeros((1, GROUP_A * tq), F32)
        _, l = lax.fori_loop(0, nch, att_body, (m0, l0))
        og = acc_ref[...] / l
        for j in range(GROUP_A):
            h = g * GROUP_A + j
            o_ref[:, h * HEAD_DIM:(h + 1) * HEAD_DIM] = og[:, j * tq:(j + 1) * tq].T.astype(o_ref.dtype)


def _dsa_attn(qit, wt, ki, q, k, vt, b, s, tq, ch, topk):
    nq = s // tq
    return pl.pallas_call(
        functools.partial(_dsa_attn_kernel, tq=tq, ch=ch, topk=topk),
        grid=(b, nq),
        in_specs=[pl.BlockSpec((1, 1, A_QI, tq), lambda bb, i: (bb, i, 0, 0)),
                  pl.BlockSpec((1, 1, N_IDX_HEADS, tq), lambda bb, i: (bb, i, 0, 0)),
                  pl.BlockSpec((s, IDX_DIM), lambda bb, i: (bb, 0)),
                  pl.BlockSpec((tq, A_Q), lambda bb, i: (bb * nq + i, 0)),
                  pl.BlockSpec((s, A_K), lambda bb, i: (bb, 0)),
                  pl.BlockSpec((1, s // ch, A_V, ch), lambda bb, i: (bb, 0, 0, 0))],
        out_specs=pl.BlockSpec((tq, A_Q), lambda bb, i: (bb * nq + i, 0)),
        out_shape=jax.ShapeDtypeStruct((b * s, A_Q), BF16),
        scratch_shapes=[pltpu.VMEM((s, tq), jnp.int32),
                        pltpu.VMEM((1, tq), jnp.int32),
                        pltpu.VMEM((1, tq), jnp.int32),
                        pltpu.VMEM((HEAD_DIM, GROUP_A * tq), F32)],
        compiler_params=_params(("parallel", "arbitrary")),
        name="dsa_attn",
    )(qit, wt, ki, q, k, vt)


def _sb_attn_kernel(q_ref, k_ref, vt_ref, o_ref, acc_ref, *, tq):
    i = pl.program_id(2)
    ch = tq
    qb = q_ref[...]
    t_row = i * tq + lax.broadcasted_iota(jnp.int32, (1, tq), 1)
    r = lax.broadcasted_iota(jnp.int32, (ch, ch), 0)
    cc = lax.broadcasted_iota(jnp.int32, (ch, ch), 1)
    tri = jnp.where(cc >= r, 1.0, 0.0).astype(BF16)
    acc_ref[...] = jnp.zeros(acc_ref.shape, F32)

    def body(j, run):
        c = i - j
        s0 = pl.multiple_of(c * ch, ch)
        z = lax.dot_general(k_ref[pl.ds(s0, ch), :], qb, NT_DIMS, preferred_element_type=F32)
        strict = (s0 + lax.broadcasted_iota(jnp.int32, (ch, tq), 0)) < t_row
        sp = jnp.maximum(z, 0.0) + jnp.log(1.0 + jnp.exp(-jnp.abs(z)))
        sp = jnp.where(strict, sp, 0.0)
        hi = sp.astype(BF16)
        lo = (sp - hi.astype(F32)).astype(BF16)
        cum = (jnp.dot(tri, hi, preferred_element_type=F32)
               + jnp.dot(tri, lo, preferred_element_type=F32) + run)
        a = jnp.where(strict, jnp.exp(z - cum), 0.0)
        acc_ref[...] += jnp.dot(vt_ref[0, c], a.astype(BF16), preferred_element_type=F32)
        return run + jnp.sum(sp, axis=0, keepdims=True)

    lax.fori_loop(0, i + 1, body, jnp.zeros((1, tq), F32))
    o_ref[...] = acc_ref[...].T.astype(o_ref.dtype)


def _sb_attn(q, k, vt, b, s, tq):
    nq = s // tq
    d = N_HEADS_B * HEAD_DIM
    return pl.pallas_call(
        functools.partial(_sb_attn_kernel, tq=tq),
        grid=(b, N_HEADS_B, nq),
        in_specs=[pl.BlockSpec((tq, HEAD_DIM), lambda bb, h, i: (bb * nq + i, h)),
                  pl.BlockSpec((s, HEAD_DIM), lambda bb, h, i: (bb, h)),
                  pl.BlockSpec((1, s // tq, HEAD_DIM, tq), lambda bb, h, i: (bb, 0, h, 0))],
        out_specs=pl.BlockSpec((tq, HEAD_DIM), lambda bb, h, i: (bb * nq + i, h)),
        out_shape=jax.ShapeDtypeStruct((b * s, d), BF16),
        scratch_shapes=[pltpu.VMEM((HEAD_DIM, tq), F32)],
        compiler_params=_params(("parallel", "parallel", "arbitrary")),
        name="sb_attn",
    )(q, k, vt)


def _transpose_chunks_kernel(x_ref, o_ref):
    o_ref[0, 0] = x_ref[...].astype(F32).T.astype(o_ref.dtype)


def _transpose_chunks(x, d, col, b, s, ch):
    n = x.shape[0]
    nb = s // ch
    return pl.pallas_call(
        _transpose_chunks_kernel,
        grid=(n // ch,),
        in_specs=[pl.BlockSpec((ch, d), lambda i: (i, col))],
        out_specs=pl.BlockSpec((1, 1, d, ch), lambda i: (i // nb, i % nb, 0, 0)),
        out_shape=jax.ShapeDtypeStruct((b, nb, d, ch), x.dtype),
        compiler_params=_params(("parallel",)),
        name="transpose_chunks",
    )(x)


def _silu_mul(g, up):
    return g * (1.0 / (1.0 + jnp.exp(-g))) * up


def _ffn_kernel(x_ref, nw_ref, wg_ref, wu_ref, wd_ref, o_ref, u_ref):
    j = pl.program_id(1)

    @pl.when(j == 0)
    def _():
        x = x_ref[...]
        u_ref[...] = _rms(x, nw_ref[...]).astype(BF16)
        o_ref[...] = x

    u = u_ref[...]
    g = jnp.dot(u, wg_ref[...], preferred_element_type=F32)
    up = jnp.dot(u, wu_ref[...], preferred_element_type=F32)
    o_ref[...] += jnp.dot(_silu_mul(g, up).astype(BF16), wd_ref[...], preferred_element_type=F32)


def _ffn_dense(x, nw, wgu, wd, bm=512, fc=1408):
    n, d = x.shape
    nj = D_FF // fc
    return pl.pallas_call(
        _ffn_kernel,
        grid=(n // bm, nj),
        in_specs=[pl.BlockSpec((bm, d), lambda i, j: (i, 0)),
                  pl.BlockSpec((1, d), lambda i, j: (0, 0)),
                  pl.BlockSpec((d, fc), lambda i, j: (0, j)),
                  pl.BlockSpec((d, fc), lambda i, j: (0, j + nj)),
                  pl.BlockSpec((fc, d), lambda i, j: (j, 0))],
        out_specs=pl.BlockSpec((bm, d), lambda i, j: (i, 0)),
        out_shape=jax.ShapeDtypeStruct((n, d), F32),
        scratch_shapes=[pltpu.VMEM((bm, d), BF16)],
        compiler_params=_params(("parallel", "arbitrary")),
        name="ffn_dense",
    )(x, nw.reshape(1, d), wgu, wgu, wd)


def _router_kernel(x_ref, nw_ref, rwt_ref, g_ref):
    u = _rms(x_ref[...], nw_ref[...])
    lt = lax.dot_general(rwt_ref[...], u, NT_DIMS, preferred_element_type=F32,
                         precision=lax.Precision.HIGHEST)
    e = lax.broadcasted_iota(jnp.int32, lt.shape, 0)
    m1 = jnp.max(lt, axis=0, keepdims=True)
    i1 = jnp.min(jnp.where(lt == m1, e, N_EXPERTS), axis=0, keepdims=True)
    rest = jnp.where(e == i1, -jnp.inf, lt)
    m2 = jnp.max(rest, axis=0, keepdims=True)
    i2 = jnp.min(jnp.where(rest == m2, e, N_EXPERTS), axis=0, keepdims=True)
    w2 = jnp.exp(m2 - m1)
    den = 1.0 + w2
    g_ref[...] = jnp.where(e == i1, 1.0 / den, 0.0) + jnp.where(e == i2, w2 / den, 0.0)


def _router(x, nw, rwt, bm=512):
    n, d = x.shape
    return pl.pallas_call(
        _router_kernel,
        grid=(n // bm,),
        in_specs=[pl.BlockSpec((bm, d), lambda i: (i, 0)),
                  pl.BlockSpec((1, d), lambda i: (0, 0)),
                  pl.BlockSpec((N_EXPERTS, d), lambda i: (0, 0))],
        out_specs=pl.BlockSpec((N_EXPERTS, bm), lambda i: (0, i)),
        out_shape=jax.ShapeDtypeStruct((N_EXPERTS, n), F32),
        compiler_params=_params(("parallel",)),
        name="moe_router",
    )(x, nw.reshape(1, d), rwt)


def _moe_kernel(x_ref, nw_ref, g_ref, wg_ref, wu_ref, wd_ref, o_ref, u_ref):
    e = pl.program_id(1)
    j = pl.program_id(2)

    @pl.when(jnp.logical_and(e == 0, j == 0))
    def _():
        x = x_ref[...]
        u_ref[...] = _rms(x, nw_ref[...]).astype(BF16)
        o_ref[...] = x

    u = u_ref[...]
    g = jnp.dot(u, wg_ref[0], preferred_element_type=F32)
    up = jnp.dot(u, wu_ref[0], preferred_element_type=F32)
    lane = lax.broadcasted_iota(jnp.int32, g_ref.shape, 1)
    gate = jnp.sum(jnp.where(lane == e, g_ref[...], 0.0), axis=1, keepdims=True)
    y = jnp.dot(_silu_mul(g, up).astype(BF16), wd_ref[0], preferred_element_type=F32)
    o_ref[...] += gate * y


def _moe_dense(x, nw, gates, wgu, wd, bm=512, fc=1408):
    n, d = x.shape
    nj = D_FF // fc
    return pl.pallas_call(
        _moe_kernel,
        grid=(n // bm, N_EXPERTS, nj),
        in_specs=[pl.BlockSpec((bm, d), lambda i, e, j: (i, 0)),
                  pl.BlockSpec((1, d), lambda i, e, j: (0, 0)),
                  pl.BlockSpec((bm, N_EXPERTS), lambda i, e, j: (i, 0)),
                  pl.BlockSpec((1, d, fc), lambda i, e, j: (e, 0, j)),
                  pl.BlockSpec((1, d, fc), lambda i, e, j: (e, 0, j + nj)),
                  pl.BlockSpec((1, fc, d), lambda i, e, j: (e, j, 0))],
        out_specs=pl.BlockSpec((bm, d), lambda i, e, j: (i, 0)),
        out_shape=jax.ShapeDtypeStruct((n, d), F32),
        scratch_shapes=[pltpu.VMEM((bm, d), BF16)],
        compiler_params=_params(("parallel", "arbitrary", "arbitrary")),
        name="moe_dense",
    )(x, nw.reshape(1, d), gates, wgu, wgu, wd)


def kernel(x, a_norm_w, a_w_in, a_q_norm_w, a_k_norm_w, a_w_out, kv_norm_w, kv_w, b_norm_w, b_w_q, b_w_out,
           ffn_norm_w, dense_w_gate_up, dense_w_down, moe_router_w, moe_w_gate_up, moe_w_down):
    b, s, d = x.shape
    n = b * s
    topk = min(MAX_TOPK, s // 4)
    tq_a, ch_a, tq_b = 128, 256, 256
    h = x.reshape(n, d)

    w_in = jnp.pad(a_w_in[0], ((0, 0), (0, A_IN_PAD - A_IN))).astype(BF16)
    proj = _norm_matmul(h, a_norm_w[0], w_in, F32)
    q, k, vt, qit, ki, wt = _dsa_prep(proj, a_q_norm_w[0], a_k_norm_w[0], b, s, tq_a, ch_a)
    o = _dsa_attn(qit, wt, ki, q, k, vt, b, s, tq_a, ch_a, topk)
    h = _matmul_residual(o, a_w_out[0].astype(BF16), h)
    h = _ffn_dense(h, ffn_norm_w[0], dense_w_gate_up[0].astype(BF16), dense_w_down[0].astype(BF16))

    kv = _norm_matmul(h, kv_norm_w, kv_w.astype(BF16), BF16)
    vt_sb = _transpose_chunks(kv, d, 1, b, s, tq_b)

    q_sb = _norm_matmul(h, b_norm_w[0], (b_w_q[0] * (HEAD_DIM ** -0.5)).astype(BF16), BF16)
    o = _sb_attn(q_sb, kv, vt_sb, b, s, tq_b)
    h = _matmul_residual(o, b_w_out[0].astype(BF16), h)
    gates = _router(h, ffn_norm_w[1], moe_router_w[0].T)
    h = _moe_dense(h, ffn_norm_w[1], gates.T, moe_w_gate_up[0].astype(BF16), moe_w_down[0].astype(BF16))
    return h.reshape(b, s, d)
```

```python
import functools

import jax
import jax.numpy as jnp
from jax import lax
from jax.experimental import pallas as pl
from jax.experimental.pallas import tpu as pltpu

D_MODEL = 1024
HEAD_DIM = 128
N_HEADS_A = 8
N_KV_HEADS_A = 2
GROUP_A = 4
N_IDX_HEADS = 8
IDX_DIM = 64
MAX_TOPK = 256
N_HEADS_B = 8
ROPE_THETA = 500000.0
ROT_FRAC = 4
NORM_EPS = 1e-6
D_FF = 2816
N_EXPERTS = 8

A_Q = N_HEADS_A * HEAD_DIM
A_K = N_KV_HEADS_A * HEAD_DIM
A_V = N_KV_HEADS_A * HEAD_DIM
A_QI = N_IDX_HEADS * IDX_DIM
A_IN = A_Q + A_K + A_V + A_QI + IDX_DIM + N_IDX_HEADS
A_IN_PAD = A_Q + A_K + A_V + A_QI + 128

LANE = 128
INT_MIN = -(2 ** 31)
LOG2E = 1.4426950408889634
SB_DEAD_LOG2 = 151.0
VMEM_LIMIT = 56 * 1024 * 1024

F32 = jnp.float32
BF16 = jnp.bfloat16
NT_DIMS = (((1,), (1,)), ((), ()))


def _params(sem):
    return pltpu.CompilerParams(dimension_semantics=sem, vmem_limit_bytes=VMEM_LIMIT)


def _rms(x, w):
    ms = jnp.mean(x * x, axis=-1, keepdims=True)
    return x * lax.rsqrt(ms + NORM_EPS) * w


def _norm_matmul_kernel(x_ref, nw_ref, w_ref, o_ref):
    u = _rms(x_ref[...], nw_ref[...]).astype(BF16)
    o_ref[...] = jnp.dot(u, w_ref[...], preferred_element_type=F32).astype(o_ref.dtype)


def _norm_matmul(x, nw, w, out_dtype, bm=512):
    n, d = x.shape
    m = w.shape[1]
    return pl.pallas_call(
        _norm_matmul_kernel,
        grid=(n // bm,),
        in_specs=[pl.BlockSpec((bm, d), lambda i: (i, 0)),
                  pl.BlockSpec((1, d), lambda i: (0, 0)),
                  pl.BlockSpec((d, m), lambda i: (0, 0))],
        out_specs=pl.BlockSpec((bm, m), lambda i: (i, 0)),
        out_shape=jax.ShapeDtypeStruct((n, m), out_dtype),
        compiler_params=_params(("parallel",)),
        name="norm_matmul",
    )(x, nw.reshape(1, d), w)


def _matmul_res_kernel(a_ref, w_ref, r_ref, o_ref):
    o_ref[...] = r_ref[...] + jnp.dot(a_ref[...], w_ref[...], preferred_element_type=F32)


def _matmul_residual(a, w, res, bm=512):
    n, k = a.shape
    m = w.shape[1]
    return pl.pallas_call(
        _matmul_res_kernel,
        grid=(n // bm,),
        in_specs=[pl.BlockSpec((bm, k), lambda i: (i, 0)),
                  pl.BlockSpec((k, m), lambda i: (0, 0)),
                  pl.BlockSpec((bm, m), lambda i: (i, 0))],
        out_specs=pl.BlockSpec((bm, m), lambda i: (i, 0)),
        out_shape=jax.ShapeDtypeStruct((n, m), F32),
        compiler_params=_params(("parallel",)),
        name="matmul_residual",
    )(a, w, res)


def _rope(x, c, s_lo, s_hi, half):
    n = x.shape[-1]
    return x * c + pltpu.roll(x, n - half, 1) * s_lo + pltpu.roll(x, half, 1) * s_hi


def _dsa_prep_kernel(p_ref, qw_ref, kw_ref, c_ref, slo_ref, shi_ref, ci_ref, sloi_ref, shii_ref,
                     q_ref, k_ref, vt_ref, qit_ref, ki_ref, wt_ref, *, tq):
    bm = p_ref.shape[0]
    c, slo, shi = c_ref[...], slo_ref[...], shi_ref[...]
    ci, sloi, shii = ci_ref[...], sloi_ref[...], shii_ref[...]
    scale = HEAD_DIM ** -0.5 * LOG2E
    for h in range(N_HEADS_A):
        xh = _rms(p_ref[:, h * HEAD_DIM:(h + 1) * HEAD_DIM], qw_ref[...])
        q_ref[:, h * HEAD_DIM:(h + 1) * HEAD_DIM] = (_rope(xh, c, slo, shi, 16) * scale).astype(BF16)
    for h in range(N_KV_HEADS_A):
        xh = _rms(p_ref[:, A_Q + h * HEAD_DIM:A_Q + (h + 1) * HEAD_DIM], kw_ref[...])
        k_ref[:, h * HEAD_DIM:(h + 1) * HEAD_DIM] = _rope(xh, c, slo, shi, 16).astype(BF16)
    v = p_ref[:, A_Q + A_K:A_Q + A_K + A_V]
    vt_ref[0, 0] = v.T.astype(BF16)
    off = A_Q + A_K + A_V
    for j in range(A_QI // LANE):
        xj = _rope(p_ref[:, off + j * LANE:off + (j + 1) * LANE], ci, sloi, shii, 8)
        xt = xj.T.astype(BF16)
        for b in range(bm // tq):
            qit_ref[0, b, j * LANE:(j + 1) * LANE, :] = xt[:, b * tq:(b + 1) * tq]
    off += A_QI
    lane = lax.broadcasted_iota(jnp.int32, (1, LANE), 1)
    first = lane < IDX_DIM
    kw = p_ref[:, off:off + LANE]
    kw = _rope(kw, jnp.where(first, ci, 1.0), jnp.where(first, sloi, 0.0), jnp.where(first, shii, 0.0), 8)
    ki_ref[...] = kw[:, :IDX_DIM].astype(BF16)
    wt = kw.T[IDX_DIM:IDX_DIM + N_IDX_HEADS, :] * (N_IDX_HEADS ** -0.5) * (IDX_DIM ** -0.5)
    for b in range(bm // tq):
        wt_ref[0, b] = wt[:, b * tq:(b + 1) * tq]


def _rope_tables(s, head, reps):
    r = head // ROT_FRAC
    half = r // 2
    inv = ROPE_THETA ** (-jnp.arange(half, dtype=F32) * 2.0 / r)
    ang = jnp.arange(s).astype(F32)[:, None] * inv[None, :]
    cos, sin = jnp.cos(ang), jnp.sin(ang)
    one = jnp.ones((s, head - r), F32)
    zero = jnp.zeros((s, head - r), F32)
    zh = jnp.zeros((s, half), F32)
    c = jnp.concatenate([cos, cos, one], axis=1)
    s_lo = jnp.concatenate([-sin, zh, zero], axis=1)
    s_hi = jnp.concatenate([zh, sin, zero], axis=1)
    return tuple(jnp.tile(t, (1, reps)) for t in (c, s_lo, s_hi))


def _dsa_prep(proj, qw, kw, b, s, tq, ch):
    n = proj.shape[0]
    bm = ch
    nb = s // bm
    c, slo, shi = _rope_tables(s, HEAD_DIM, 1)
    ci, sloi, shii = _rope_tables(s, IDX_DIM, 2)
    tab = pl.BlockSpec((bm, LANE), lambda i: (i % nb, 0))
    row = lambda w: pl.BlockSpec((bm, w), lambda i: (i, 0))
    out_shapes = (
        jax.ShapeDtypeStruct((n, A_Q), BF16),
        jax.ShapeDtypeStruct((n, A_K), BF16),
        jax.ShapeDtypeStruct((b, s // ch, A_V, ch), BF16),
        jax.ShapeDtypeStruct((b, s // tq, A_QI, tq), BF16),
        jax.ShapeDtypeStruct((n, IDX_DIM), BF16),
        jax.ShapeDtypeStruct((b, s // tq, N_IDX_HEADS, tq), F32),
    )
    out_specs = (
        row(A_Q), row(A_K),
        pl.BlockSpec((1, 1, A_V, ch), lambda i: (i // nb, i % nb, 0, 0)),
        pl.BlockSpec((1, bm // tq, A_QI, tq), lambda i: (i // nb, i % nb, 0, 0)),
        row(IDX_DIM),
        pl.BlockSpec((1, bm // tq, N_IDX_HEADS, tq), lambda i: (i // nb, i % nb, 0, 0)),
    )
    return pl.pallas_call(
        functools.partial(_dsa_prep_kernel, tq=tq),
        grid=(n // bm,),
        in_specs=[row(A_IN_PAD),
                  pl.BlockSpec((1, HEAD_DIM), lambda i: (0, 0)),
                  pl.BlockSpec((1, HEAD_DIM), lambda i: (0, 0)),
                  tab, tab, tab, tab, tab, tab],
        out_specs=out_specs,
        out_shape=out_shapes,
        compiler_params=_params(("parallel",)),
        name="dsa_prep",
    )(proj, qw.reshape(1, -1), kw.reshape(1, -1), c, slo, shi, ci, sloi, shii)


def _dsa_attn_kernel(qit_ref, wt_ref, ki_ref, q_ref, k_ref, vt_ref, o_ref,
                     keys_ref, thr_ref, cnt_ref, acc_ref, *, tq, ch, topk):
    i = pl.program_id(1)
    nch = ((i + 1) * tq + ch - 1) // ch
    t_row = i * tq + lax.broadcasted_iota(jnp.int32, (1, tq), 1)

    qit = jnp.concatenate([qit_ref[0, 0, h * IDX_DIM:(h + 1) * IDX_DIM, :] for h in range(N_IDX_HEADS)],
                          axis=1)
    wt = wt_ref[0, 0]

    def score_body(c, carry):
        s0 = pl.multiple_of(c * ch, ch)
        rel = jnp.dot(ki_ref[pl.ds(s0, ch), :], qit, preferred_element_type=F32)
        sc = jnp.zeros((ch, tq), F32)
        for h in range(N_IDX_HEADS):
            sc = sc + jnp.maximum(rel[:, h * tq:(h + 1) * tq], 0.0) * wt[h:h + 1, :]
        sc = jnp.where(sc == 0.0, 0.0, sc)
        bits = lax.bitcast_convert_type(sc, jnp.int32)
        key = bits ^ ((bits >> 31) & 0x7FFFFFFF)
        s_idx = s0 + lax.broadcasted_iota(jnp.int32, (ch, tq), 0)
        keys_ref[pl.ds(s0, ch), :] = jnp.where(s_idx <= t_row, key, INT_MIN)
        return carry

    lax.fori_loop(0, nch, score_body, 0)

    def count_ge(thr):
        def body(c, acc):
            s0 = pl.multiple_of(c * ch, ch)
            hit = jnp.where(keys_ref[pl.ds(s0, ch), :] >= thr, 1, 0)
            return acc + jnp.sum(hit.reshape(ch // 8, 8, tq), axis=0)
        acc = lax.fori_loop(0, nch, body, jnp.zeros((8, tq), jnp.int32))
        return jnp.sum(acc, axis=0, keepdims=True)

    short = t_row < topk
    c0 = count_ge(jnp.zeros((1, tq), jnp.int32))
    pos = c0 >= topk
    thr_ref[...] = jnp.where(pos, 0, INT_MIN)
    cnt_ref[...] = jnp.where(pos, c0, -1)

    def open_rows():
        return jnp.logical_and(jnp.logical_not(short), cnt_ref[...] != topk)

    def n_open():
        return jnp.sum(jnp.where(open_rows(), 1, 0))

    def bis_cond(st):
        bit, n = st
        return jnp.logical_and(bit >= 0, n > 0)

    def bis_body(st):
        bit, _ = st
        cand = thr_ref[...] + (jnp.int32(1) << bit)
        c = count_ge(cand)
        take = jnp.logical_and(c >= topk, open_rows())
        thr_ref[...] = jnp.where(take, cand, thr_ref[...])
        cnt_ref[...] = jnp.where(take, c, cnt_ref[...])
        return bit - 1, n_open()

    _, left = lax.while_loop(bis_cond, bis_body, (jnp.int32(30), n_open()))

    @pl.when(left > 0)
    def _():
        thr = jnp.where(open_rows(), thr_ref[...], INT_MIN)

        def gt_body(c, acc):
            s0 = pl.multiple_of(c * ch, ch)
            hit = jnp.where(keys_ref[pl.ds(s0, ch), :] > thr, 1, 0)
            return acc + jnp.sum(hit.reshape(ch // 8, 8, tq), axis=0)
        gt = jnp.sum(lax.fori_loop(0, nch, gt_body, jnp.zeros((8, tq), jnp.int32)), axis=0, keepdims=True)
        need = (topk - gt).astype(F32)
        r = lax.broadcasted_iota(jnp.int32, (ch, ch), 0)
        cc = lax.broadcasted_iota(jnp.int32, (ch, ch), 1)
        tri = jnp.where(cc <= r, 1.0, 0.0).astype(BF16)

        def tie_body(c, run):
            s0 = pl.multiple_of(c * ch, ch)
            kc = keys_ref[pl.ds(s0, ch), :]
            eq = kc == thr
            eqf = jnp.where(eq, 1.0, 0.0)
            pre = jnp.dot(tri, eqf.astype(BF16), preferred_element_type=F32) + run
            drop = jnp.logical_and(eq, pre > need)
            keys_ref[pl.ds(s0, ch), :] = jnp.where(drop, INT_MIN, kc)
            return run + jnp.sum(eqf, axis=0, keepdims=True)

        lax.fori_loop(0, nch, tie_body, jnp.zeros((1, tq), F32))

    thr = jnp.where(short, INT_MIN + 1, thr_ref[...])

    qs = [jnp.concatenate([q_ref[:, (g * GROUP_A + j) * HEAD_DIM:(g * GROUP_A + j + 1) * HEAD_DIM]
                           for j in range(GROUP_A)], axis=0) for g in range(N_KV_HEADS_A)]
    acc_ref[...] = jnp.zeros(acc_ref.shape, F32)

    def att_body(c, carry):
        s0 = pl.multiple_of(c * ch, ch)
        bias = jnp.where(keys_ref[pl.ds(s0, ch), :] >= thr, 0.0, -jnp.inf)
        bias = jnp.concatenate([bias] * GROUP_A, axis=1)
        out = []
        for g in range(N_KV_HEADS_A):
            m, l = carry[g]
            kc = k_ref[pl.ds(s0, ch), g * HEAD_DIM:(g + 1) * HEAD_DIM]
            lg = lax.dot_general(kc, qs[g], NT_DIMS, preferred_element_type=F32) + bias
            m_new = jnp.maximum(m, jnp.max(lg, axis=0, keepdims=True))
            m_safe = jnp.where(m_new == -jnp.inf, 0.0, m_new)
            p = jnp.exp2(lg - m_safe)
            alpha = jnp.exp2(m - m_safe)
            l = alpha * l + jnp.sum(p, axis=0, keepdims=True)
            pv = jnp.dot(vt_ref[0, c, g * HEAD_DIM:(g + 1) * HEAD_DIM, :], p.astype(BF16),
                         preferred_element_type=F32)
            acc_ref[g] = acc_ref[g] * alpha + pv
            out.append((m_new, l))
        return tuple(out)

    m0 = jnp.full((1, GROUP_A * tq), -jnp.inf, F32)
    l0 = jnp.zeros((1, GROUP_A * tq), F32)
    fin = lax.fori_loop(0, nch, att_body, ((m0, l0),) * N_KV_HEADS_A)
    for g in range(N_KV_HEADS_A):
        og = acc_ref[g] / fin[g][1]
        for j in range(GROUP_A):
            h = g * GROUP_A + j
            o_ref[:, h * HEAD_DIM:(h + 1) * HEAD_DIM] = og[:, j * tq:(j + 1) * tq].T.astype(o_ref.dtype)


def _dsa_attn(qit, wt, ki, q, k, vt, b, s, tq, ch, topk):
    nq = s // tq
    return pl.pallas_call(
        functools.partial(_dsa_attn_kernel, tq=tq, ch=ch, topk=topk),
        grid=(b, nq),
        in_specs=[pl.BlockSpec((1, 1, A_QI, tq), lambda bb, i: (bb, i, 0, 0)),
                  pl.BlockSpec((1, 1, N_IDX_HEADS, tq), lambda bb, i: (bb, i, 0, 0)),
                  pl.BlockSpec((s, IDX_DIM), lambda bb, i: (bb, 0)),
                  pl.BlockSpec((tq, A_Q), lambda bb, i: (bb * nq + i, 0)),
                  pl.BlockSpec((s, A_K), lambda bb, i: (bb, 0)),
                  pl.BlockSpec((1, s // ch, A_V, ch), lambda bb, i: (bb, 0, 0, 0))],
        out_specs=pl.BlockSpec((tq, A_Q), lambda bb, i: (bb * nq + i, 0)),
        out_shape=jax.ShapeDtypeStruct((b * s, A_Q), BF16),
        scratch_shapes=[pltpu.VMEM((s, tq), jnp.int32),
                        pltpu.VMEM((1, tq), jnp.int32),
                        pltpu.VMEM((1, tq), jnp.int32),
                        pltpu.VMEM((N_KV_HEADS_A, HEAD_DIM, GROUP_A * tq), F32)],
        compiler_params=_params(("parallel", "arbitrary")),
        name="dsa_attn",
    )(qit, wt, ki, q, k, vt)


def _sb_attn_kernel(q_ref, k_ref, vt_ref, o_ref, acc_ref, *, tq, hb):
    i = pl.program_id(2)
    ch = tq
    t_row = i * tq + lax.broadcasted_iota(jnp.int32, (1, tq), 1)
    r = lax.broadcasted_iota(jnp.int32, (ch, ch), 0)
    cc = lax.broadcasted_iota(jnp.int32, (ch, ch), 1)
    tri = jnp.where(cc >= r, 1.0, 0.0).astype(BF16)
    acc_ref[...] = jnp.zeros(acc_ref.shape, F32)

    def cond(st):
        j, _, low = st
        return jnp.logical_and(j <= i, low <= SB_DEAD_LOG2)

    def body(st):
        j, runs, _ = st
        c = i - j
        s0 = pl.multiple_of(c * ch, ch)
        strict = (s0 + lax.broadcasted_iota(jnp.int32, (ch, tq), 0)) < t_row
        new_runs = []
        for h in range(hb):
            hs = slice(h * HEAD_DIM, (h + 1) * HEAD_DIM)
            z = lax.dot_general(k_ref[pl.ds(s0, ch), hs], q_ref[:, hs], NT_DIMS,
                                preferred_element_type=F32)
            sp = jnp.maximum(z, 0.0) + jnp.log(1.0 + jnp.exp2(-jnp.abs(z))) * LOG2E
            sp = jnp.where(strict, sp, 0.0)
            hi = sp.astype(BF16)
            lo = (sp - hi.astype(F32)).astype(BF16)
            cum = (jnp.dot(tri, hi, preferred_element_type=F32)
                   + jnp.dot(tri, lo, preferred_element_type=F32) + runs[h])
            a = jnp.where(strict, jnp.exp2(z - cum), 0.0)
            acc_ref[h] += jnp.dot(vt_ref[0, c, hs, :], a.astype(BF16), preferred_element_type=F32)
            new_runs.append(runs[h] + jnp.sum(sp, axis=0, keepdims=True))
        low = jnp.min(functools.reduce(jnp.minimum, new_runs))
        return j + 1, tuple(new_runs), low

    run0 = tuple(jnp.zeros((1, tq), F32) for _ in range(hb))
    lax.while_loop(cond, body, (jnp.int32(0), run0, jnp.float32(0.0)))
    for h in range(hb):
        o_ref[:, h * HEAD_DIM:(h + 1) * HEAD_DIM] = acc_ref[h].T.astype(o_ref.dtype)


def _sb_attn(q, k, vt, b, s, tq, hb=4):
    nq = s // tq
    d = N_HEADS_B * HEAD_DIM
    w = hb * HEAD_DIM
    return pl.pallas_call(
        functools.partial(_sb_attn_kernel, tq=tq, hb=hb),
        grid=(b, N_HEADS_B // hb, nq),
        in_specs=[pl.BlockSpec((tq, w), lambda bb, h, i: (bb * nq + i, h)),
                  pl.BlockSpec((s, w), lambda bb, h, i: (bb, h)),
                  pl.BlockSpec((1, s // tq, w, tq), lambda bb, h, i: (bb, 0, h, 0))],
        out_specs=pl.BlockSpec((tq, w), lambda bb, h, i: (bb * nq + i, h)),
        out_shape=jax.ShapeDtypeStruct((b * s, d), BF16),
        scratch_shapes=[pltpu.VMEM((hb, HEAD_DIM, tq), F32)],
        compiler_params=_params(("parallel", "parallel", "arbitrary")),
        name="sb_attn",
    )(q, k, vt)


def _transpose_chunks_kernel(x_ref, o_ref):
    o_ref[0, 0] = x_ref[...].astype(F32).T.astype(o_ref.dtype)


def _transpose_chunks(x, d, col, b, s, ch):
    n = x.shape[0]
    nb = s // ch
    return pl.pallas_call(
        _transpose_chunks_kernel,
        grid=(n // ch,),
        in_specs=[pl.BlockSpec((ch, d), lambda i: (i, col))],
        out_specs=pl.BlockSpec((1, 1, d, ch), lambda i: (i // nb, i % nb, 0, 0)),
        out_shape=jax.ShapeDtypeStruct((b, nb, d, ch), x.dtype),
        compiler_params=_params(("parallel",)),
        name="transpose_chunks",
    )(x)


def _silu_mul(g, up):
    return g * (1.0 / (1.0 + jnp.exp(-g))) * up


def _ffn_kernel(x_ref, nw_ref, wg_ref, wu_ref, wd_ref, o_ref, u_ref):
    j = pl.program_id(1)

    @pl.when(j == 0)
    def _():
        x = x_ref[...]
        u_ref[...] = _rms(x, nw_ref[...]).astype(BF16)
        o_ref[...] = x

    u = u_ref[...]
    g = jnp.dot(u, wg_ref[...], preferred_element_type=F32)
    up = jnp.dot(u, wu_ref[...], preferred_element_type=F32)
    o_ref[...] += jnp.dot(_silu_mul(g, up).astype(BF16), wd_ref[...], preferred_element_type=F32)


def _ffn_dense(x, nw, wgu, wd, bm=512, fc=1408):
    n, d = x.shape
    nj = D_FF // fc
    return pl.pallas_call(
        _ffn_kernel,
        grid=(n // bm, nj),
        in_specs=[pl.BlockSpec((bm, d), lambda i, j: (i, 0)),
                  pl.BlockSpec((1, d), lambda i, j: (0, 0)),
                  pl.BlockSpec((d, fc), lambda i, j: (0, j)),
                  pl.BlockSpec((d, fc), lambda i, j: (0, j + nj)),
                  pl.BlockSpec((fc, d), lambda i, j: (j, 0))],
        out_specs=pl.BlockSpec((bm, d), lambda i, j: (i, 0)),
        out_shape=jax.ShapeDtypeStruct((n, d), F32),
        scratch_shapes=[pltpu.VMEM((bm, d), BF16)],
        compiler_params=_params(("parallel", "arbitrary")),
        name="ffn_dense",
    )(x, nw.reshape(1, d), wgu, wgu, wd)


def _router_kernel(x_ref, nw_ref, rwt_ref, g_ref):
    u = _rms(x_ref[...], nw_ref[...])
    lt = lax.dot_general(rwt_ref[...], u, NT_DIMS, preferred_element_type=F32,
                         precision=lax.Precision.HIGHEST)
    e = lax.broadcasted_iota(jnp.int32, lt.shape, 0)
    m1 = jnp.max(lt, axis=0, keepdims=True)
    i1 = jnp.min(jnp.where(lt == m1, e, N_EXPERTS), axis=0, keepdims=True)
    rest = jnp.where(e == i1, -jnp.inf, lt)
    m2 = jnp.max(rest, axis=0, keepdims=True)
    i2 = jnp.min(jnp.where(rest == m2, e, N_EXPERTS), axis=0, keepdims=True)
    w2 = jnp.exp(m2 - m1)
    den = 1.0 + w2
    g_ref[...] = jnp.where(e == i1, 1.0 / den, 0.0) + jnp.where(e == i2, w2 / den, 0.0)


def _router(x, nw, rwt, bm=512):
    n, d = x.shape
    return pl.pallas_call(
        _router_kernel,
        grid=(n // bm,),
        in_specs=[pl.BlockSpec((bm, d), lambda i: (i, 0)),
                  pl.BlockSpec((1, d), lambda i: (0, 0)),
                  pl.BlockSpec((N_EXPERTS, d), lambda i: (0, 0))],
        out_specs=pl.BlockSpec((N_EXPERTS, bm), lambda i: (0, i)),
        out_shape=jax.ShapeDtypeStruct((N_EXPERTS, n), F32),
        compiler_params=_params(("parallel",)),
        name="moe_router",
    )(x, nw.reshape(1, d), rwt)


def _moe_kernel(x_ref, nw_ref, g_ref, wg_ref, wu_ref, wd_ref, o_ref, u_ref):
    e = pl.program_id(1)
    j = pl.program_id(2)

    @pl.when(jnp.logical_and(e == 0, j == 0))
    def _():
        x = x_ref[...]
        u_ref[...] = _rms(x, nw_ref[...]).astype(BF16)
        o_ref[...] = x

    u = u_ref[...]
    g = jnp.dot(u, wg_ref[0], preferred_element_type=F32)
    up = jnp.dot(u, wu_ref[0], preferred_element_type=F32)
    lane = lax.broadcasted_iota(jnp.int32, g_ref.shape, 1)
    gate = jnp.sum(jnp.where(lane == e, g_ref[...], 0.0), axis=1, keepdims=True)
    y = jnp.dot(_silu_mul(g, up).astype(BF16), wd_ref[0], preferred_element_type=F32)
    o_ref[...] += gate * y


def _moe_dense(x, nw, gates, wgu, wd, bm=512, fc=1408):
    n, d = x.shape
    nj = D_FF // fc
    return pl.pallas_call(
        _moe_kernel,
        grid=(n // bm, N_EXPERTS, nj),
        in_specs=[pl.BlockSpec((bm, d), lambda i, e, j: (i, 0)),
                  pl.BlockSpec((1, d), lambda i, e, j: (0, 0)),
                  pl.BlockSpec((bm, N_EXPERTS), lambda i, e, j: (i, 0)),
                  pl.BlockSpec((1, d, fc), lambda i, e, j: (e, 0, j)),
                  pl.BlockSpec((1, d, fc), lambda i, e, j: (e, 0, j + nj)),
                  pl.BlockSpec((1, fc, d), lambda i, e, j: (e, j, 0))],
        out_specs=pl.BlockSpec((bm, d), lambda i, e, j: (i, 0)),
        out_shape=jax.ShapeDtypeStruct((n, d), F32),
        scratch_shapes=[pltpu.VMEM((bm, d), BF16)],
        compiler_params=_params(("parallel", "arbitrary", "arbitrary")),
        name="moe_dense",
    )(x, nw.reshape(1, d), gates, wgu, wgu, wd)


def kernel(x, a_norm_w, a_w_in, a_q_norm_w, a_k_norm_w, a_w_out, kv_norm_w, kv_w, b_norm_w, b_w_q, b_w_out,
           ffn_norm_w, dense_w_gate_up, dense_w_down, moe_router_w, moe_w_gate_up, moe_w_down):
    b, s, d = x.shape
    n = b * s
    topk = min(MAX_TOPK, s // 4)
    tq_a, ch_a, tq_b = 256, 256, 256
    h = x.reshape(n, d)

    w_in = jnp.pad(a_w_in[0], ((0, 0), (0, A_IN_PAD - A_IN))).astype(BF16)
    proj = _norm_matmul(h, a_norm_w[0], w_in, F32)
    q, k, vt, qit, ki, wt = _dsa_prep(proj, a_q_norm_w[0], a_k_norm_w[0], b, s, tq_a, ch_a)
    o = _dsa_attn(qit, wt, ki, q, k, vt, b, s, tq_a, ch_a, topk)
    h = _matmul_residual(o, a_w_out[0].astype(BF16), h)
    h = _ffn_dense(h, ffn_norm_w[0], dense_w_gate_up[0].astype(BF16), dense_w_down[0].astype(BF16))

    kv = _norm_matmul(h, kv_norm_w, kv_w.astype(BF16), BF16)
    vt_sb = _transpose_chunks(kv, d, 1, b, s, tq_b)

    q_sb = _norm_matmul(h, b_norm_w[0], (b_w_q[0] * (HEAD_DIM ** -0.5 * LOG2E)).astype(BF16), BF16)
    o = _sb_attn(q_sb, kv, vt_sb, b, s, tq_b)
    h = _matmul_residual(o, b_w_out[0].astype(BF16), h)
    gates = _router(h, ffn_norm_w[1], moe_router_w[0].T)
    h = _moe_dense(h, ffn_norm_w[1], gates.T, moe_w_gate_up[0].astype(BF16), moe_w_down[0].astype(BF16))
    return h.reshape(b, s, d)
```

```python
import functools

import jax
import jax.numpy as jnp
from jax import lax
from jax.experimental import pallas as pl
from jax.experimental.pallas import tpu as pltpu

D_MODEL = 1024
HEAD_DIM = 128
N_HEADS_A = 8
N_KV_HEADS_A = 2
GROUP_A = 4
N_IDX_HEADS = 8
IDX_DIM = 64
MAX_TOPK = 256
N_HEADS_B = 8
ROPE_THETA = 500000.0
ROT_FRAC = 4
NORM_EPS = 1e-6
D_FF = 2816
N_EXPERTS = 8

A_Q = N_HEADS_A * HEAD_DIM
A_K = N_KV_HEADS_A * HEAD_DIM
A_V = N_KV_HEADS_A * HEAD_DIM
A_QI = N_IDX_HEADS * IDX_DIM
A_IN = A_Q + A_K + A_V + A_QI + IDX_DIM + N_IDX_HEADS
A_IN_PAD = A_Q + A_K + A_V + A_QI + 128

LANE = 128
INT_MIN = -(2 ** 31)
HALF_MIN = -(2 ** 15)
HALF_ROWS = 16
LOG2E = 1.4426950408889634
SB_DEAD_LOG2 = 151.0
VMEM_LIMIT = 56 * 1024 * 1024

F32 = jnp.float32
BF16 = jnp.bfloat16
NT_DIMS = (((1,), (1,)), ((), ()))


def _params(sem):
    return pltpu.CompilerParams(dimension_semantics=sem, vmem_limit_bytes=VMEM_LIMIT)


def _rms(x, w):
    ms = jnp.mean(x * x, axis=-1, keepdims=True)
    return x * lax.rsqrt(ms + NORM_EPS) * w


def _norm_matmul_kernel(x_ref, nw_ref, w_ref, o_ref):
    u = _rms(x_ref[...], nw_ref[...]).astype(BF16)
    o_ref[...] = jnp.dot(u, w_ref[...], preferred_element_type=F32).astype(o_ref.dtype)


def _norm_matmul(x, nw, w, out_dtype, bm=512):
    n, d = x.shape
    m = w.shape[1]
    return pl.pallas_call(
        _norm_matmul_kernel,
        grid=(n // bm,),
        in_specs=[pl.BlockSpec((bm, d), lambda i: (i, 0)),
                  pl.BlockSpec((1, d), lambda i: (0, 0)),
                  pl.BlockSpec((d, m), lambda i: (0, 0))],
        out_specs=pl.BlockSpec((bm, m), lambda i: (i, 0)),
        out_shape=jax.ShapeDtypeStruct((n, m), out_dtype),
        compiler_params=_params(("parallel",)),
        name="norm_matmul",
    )(x, nw.reshape(1, d), w)


def _matmul_res_kernel(a_ref, w_ref, r_ref, o_ref):
    o_ref[...] = r_ref[...] + jnp.dot(a_ref[...], w_ref[...], preferred_element_type=F32)


def _matmul_residual(a, w, res, bm=512):
    n, k = a.shape
    m = w.shape[1]
    return pl.pallas_call(
        _matmul_res_kernel,
        grid=(n // bm,),
        in_specs=[pl.BlockSpec((bm, k), lambda i: (i, 0)),
                  pl.BlockSpec((k, m), lambda i: (0, 0)),
                  pl.BlockSpec((bm, m), lambda i: (i, 0))],
        out_specs=pl.BlockSpec((bm, m), lambda i: (i, 0)),
        out_shape=jax.ShapeDtypeStruct((n, m), F32),
        compiler_params=_params(("parallel",)),
        name="matmul_residual",
    )(a, w, res)


def _rope(x, c, s_lo, s_hi, half):
    n = x.shape[-1]
    return x * c + pltpu.roll(x, n - half, 1) * s_lo + pltpu.roll(x, half, 1) * s_hi


def _dsa_prep_kernel(p_ref, qw_ref, kw_ref, c_ref, slo_ref, shi_ref, ci_ref, sloi_ref, shii_ref,
                     q_ref, k_ref, vt_ref, qit_ref, ki_ref, wt_ref, *, tq):
    bm = p_ref.shape[0]
    c, slo, shi = c_ref[...], slo_ref[...], shi_ref[...]
    ci, sloi, shii = ci_ref[...], sloi_ref[...], shii_ref[...]
    scale = HEAD_DIM ** -0.5 * LOG2E
    for h in range(N_HEADS_A):
        xh = _rms(p_ref[:, h * HEAD_DIM:(h + 1) * HEAD_DIM], qw_ref[...])
        q_ref[:, h * HEAD_DIM:(h + 1) * HEAD_DIM] = (_rope(xh, c, slo, shi, 16) * scale).astype(BF16)
    for h in range(N_KV_HEADS_A):
        xh = _rms(p_ref[:, A_Q + h * HEAD_DIM:A_Q + (h + 1) * HEAD_DIM], kw_ref[...])
        k_ref[:, h * HEAD_DIM:(h + 1) * HEAD_DIM] = _rope(xh, c, slo, shi, 16).astype(BF16)
    v = p_ref[:, A_Q + A_K:A_Q + A_K + A_V]
    vt_ref[0, 0] = v.T.astype(BF16)
    off = A_Q + A_K + A_V
    for j in range(A_QI // LANE):
        xj = _rope(p_ref[:, off + j * LANE:off + (j + 1) * LANE], ci, sloi, shii, 8)
        xt = xj.T.astype(BF16)
        for b in range(bm // tq):
            qit_ref[0, b, j * LANE:(j + 1) * LANE, :] = xt[:, b * tq:(b + 1) * tq]
    off += A_QI
    lane = lax.broadcasted_iota(jnp.int32, (1, LANE), 1)
    first = lane < IDX_DIM
    kw = p_ref[:, off:off + LANE]
    kw = _rope(kw, jnp.where(first, ci, 1.0), jnp.where(first, sloi, 0.0), jnp.where(first, shii, 0.0), 8)
    ki_ref[...] = kw[:, :IDX_DIM].astype(BF16)
    wt = kw.T[IDX_DIM:IDX_DIM + N_IDX_HEADS, :] * (N_IDX_HEADS ** -0.5) * (IDX_DIM ** -0.5)
    for b in range(bm // tq):
        wt_ref[0, b] = wt[:, b * tq:(b + 1) * tq]


def _rope_tables(s, head, reps):
    r = head // ROT_FRAC
    half = r // 2
    inv = ROPE_THETA ** (-jnp.arange(half, dtype=F32) * 2.0 / r)
    ang = jnp.arange(s).astype(F32)[:, None] * inv[None, :]
    cos, sin = jnp.cos(ang), jnp.sin(ang)
    one = jnp.ones((s, head - r), F32)
    zero = jnp.zeros((s, head - r), F32)
    zh = jnp.zeros((s, half), F32)
    c = jnp.concatenate([cos, cos, one], axis=1)
    s_lo = jnp.concatenate([-sin, zh, zero], axis=1)
    s_hi = jnp.concatenate([zh, sin, zero], axis=1)
    return tuple(jnp.tile(t, (1, reps)) for t in (c, s_lo, s_hi))


def _dsa_prep(proj, qw, kw, b, s, tq, ch):
    n = proj.shape[0]
    bm = ch
    nb = s // bm
    c, slo, shi = _rope_tables(s, HEAD_DIM, 1)
    ci, sloi, shii = _rope_tables(s, IDX_DIM, 2)
    tab = pl.BlockSpec((bm, LANE), lambda i: (i % nb, 0))
    row = lambda w: pl.BlockSpec((bm, w), lambda i: (i, 0))
    out_shapes = (
        jax.ShapeDtypeStruct((n, A_Q), BF16),
        jax.ShapeDtypeStruct((n, A_K), BF16),
        jax.ShapeDtypeStruct((b, s // ch, A_V, ch), BF16),
        jax.ShapeDtypeStruct((b, s // tq, A_QI, tq), BF16),
        jax.ShapeDtypeStruct((n, IDX_DIM), BF16),
        jax.ShapeDtypeStruct((b, s // tq, N_IDX_HEADS, tq), F32),
    )
    out_specs = (
        row(A_Q), row(A_K),
        pl.BlockSpec((1, 1, A_V, ch), lambda i: (i // nb, i % nb, 0, 0)),
        pl.BlockSpec((1, bm // tq, A_QI, tq), lambda i: (i // nb, i % nb, 0, 0)),
        row(IDX_DIM),
        pl.BlockSpec((1, bm // tq, N_IDX_HEADS, tq), lambda i: (i // nb, i % nb, 0, 0)),
    )
    return pl.pallas_call(
        functools.partial(_dsa_prep_kernel, tq=tq),
        grid=(n // bm,),
        in_specs=[row(A_IN_PAD),
                  pl.BlockSpec((1, HEAD_DIM), lambda i: (0, 0)),
                  pl.BlockSpec((1, HEAD_DIM), lambda i: (0, 0)),
                  tab, tab, tab, tab, tab, tab],
        out_specs=out_specs,
        out_shape=out_shapes,
        compiler_params=_params(("parallel",)),
        name="dsa_prep",
    )(proj, qw.reshape(1, -1), kw.reshape(1, -1), c, slo, shi, ci, sloi, shii)


def _dsa_attn_kernel(qit_ref, wt_ref, ki_ref, q_ref, k_ref, vt_ref, o_ref,
                     keys_ref, hi_ref, lo_ref, thr_ref, cnt_ref, acc_ref, *, tq, ch, topk):
    i = pl.program_id(1)
    nch = ((i + 1) * tq + ch - 1) // ch
    t_row = i * tq + lax.broadcasted_iota(jnp.int32, (1, tq), 1)

    qit = jnp.concatenate([qit_ref[0, 0, h * IDX_DIM:(h + 1) * IDX_DIM, :] for h in range(N_IDX_HEADS)],
                          axis=1)
    wt = wt_ref[0, 0]

    def score_body(c, carry):
        s0 = pl.multiple_of(c * ch, ch)
        rel = jnp.dot(ki_ref[pl.ds(s0, ch), :], qit, preferred_element_type=F32)
        sc = jnp.zeros((ch, tq), F32)
        for h in range(N_IDX_HEADS):
            sc = sc + jnp.maximum(rel[:, h * tq:(h + 1) * tq], 0.0) * wt[h:h + 1, :]
        sc = jnp.where(sc == 0.0, 0.0, sc)
        bits = lax.bitcast_convert_type(sc, jnp.int32)
        key = bits ^ ((bits >> 31) & 0x7FFFFFFF)
        s_idx = s0 + lax.broadcasted_iota(jnp.int32, (ch, tq), 0)
        key = jnp.where(s_idx <= t_row, key, INT_MIN)
        keys_ref[pl.ds(s0, ch), :] = key
        hi_ref[pl.ds(s0, ch), :] = (key >> 16).astype(jnp.int16)
        return carry

    lax.fori_loop(0, nch, score_body, 0)

    def count16(ref, thr, strict):
        t16 = jnp.broadcast_to(thr.astype(jnp.int16), (HALF_ROWS, tq))

        def body(c, acc):
            s0 = pl.multiple_of(c * ch, ch)
            x = ref[pl.ds(s0, ch), :].reshape(ch // HALF_ROWS, HALF_ROWS, tq)
            hit = jnp.where((x > t16[None]) if strict else (x >= t16[None]), jnp.int16(1), jnp.int16(0))
            for r in range(ch // HALF_ROWS):
                acc = acc + hit[r]
            return acc

        acc = lax.fori_loop(0, nch, body, jnp.zeros((HALF_ROWS, tq), jnp.int16))
        return jnp.sum(acc.astype(jnp.int32), axis=0, keepdims=True)

    def bisect_pass(ref, want, bit, st):
        thr, cnt = st
        cand = thr + (jnp.int32(1) << bit)
        c = count16(ref, cand, False)
        take = c >= want
        return jnp.where(take, cand, thr), jnp.where(take, c, cnt)

    def bisect_start(ref, want, cnt_floor):
        c0 = count16(ref, jnp.zeros((1, tq), jnp.int32), False)
        pos = c0 >= want
        return jnp.where(pos, 0, HALF_MIN), jnp.where(pos, c0, cnt_floor)

    short = t_row < topk
    want_hi = jnp.full((1, tq), topk, jnp.int32)
    st = bisect_start(hi_ref, want_hi, jnp.full((1, tq), nch * ch, jnp.int32))
    thr_hi, cnt_hi = lax.fori_loop(0, 15, lambda it, s: bisect_pass(hi_ref, want_hi, 14 - it, s), st)
    above = count16(hi_ref, thr_hi, True)
    want_lo = topk - above

    def lo_body(c, carry):
        s0 = pl.multiple_of(c * ch, ch)
        key = keys_ref[pl.ds(s0, ch), :]
        lo = (key & 0xFFFF) + HALF_MIN
        lo_ref[pl.ds(s0, ch), :] = jnp.where((key >> 16) == thr_hi, lo, HALF_MIN).astype(jnp.int16)
        return carry

    lax.fori_loop(0, nch, lo_body, 0)

    def n_open(cnt):
        return jnp.sum(jnp.where(jnp.logical_and(jnp.logical_not(short), cnt != want_lo), 1, 0))

    def lo_cond(s):
        it, _, _, n = s
        return jnp.logical_and(it < 15, n > 0)

    def lo_step(s):
        it, thr, cnt, _ = s
        for k in range(3):
            thr, cnt = bisect_pass(lo_ref, want_lo, 14 - it - k, (thr, cnt))
        return it + 3, thr, cnt, n_open(cnt)

    thr_lo, cnt_lo = bisect_start(lo_ref, want_lo, cnt_hi - above)
    _, thr_lo, cnt_lo, left = lax.while_loop(lo_cond, lo_step, (jnp.int32(0), thr_lo, cnt_lo, n_open(cnt_lo)))
    thr_ref[...] = thr_hi * 65536 + (thr_lo - HALF_MIN)
    cnt_ref[...] = above + cnt_lo

    def open_rows():
        return jnp.logical_and(jnp.logical_not(short), cnt_ref[...] != topk)

    @pl.when(left > 0)
    def _():
        thr = jnp.where(open_rows(), thr_ref[...], INT_MIN)

        def gt_body(c, acc):
            s0 = pl.multiple_of(c * ch, ch)
            hit = jnp.where(keys_ref[pl.ds(s0, ch), :] > thr, 1, 0)
            return acc + jnp.sum(hit.reshape(ch // 8, 8, tq), axis=0)
        gt = jnp.sum(lax.fori_loop(0, nch, gt_body, jnp.zeros((8, tq), jnp.int32)), axis=0, keepdims=True)
        need = (topk - gt).astype(F32)
        r = lax.broadcasted_iota(jnp.int32, (ch, ch), 0)
        cc = lax.broadcasted_iota(jnp.int32, (ch, ch), 1)
        tri = jnp.where(cc <= r, 1.0, 0.0).astype(BF16)

        def tie_body(c, run):
            s0 = pl.multiple_of(c * ch, ch)
            kc = keys_ref[pl.ds(s0, ch), :]
            eq = kc == thr
            eqf = jnp.where(eq, 1.0, 0.0)
            pre = jnp.dot(tri, eqf.astype(BF16), preferred_element_type=F32) + run
            drop = jnp.logical_and(eq, pre > need)
            keys_ref[pl.ds(s0, ch), :] = jnp.where(drop, INT_MIN, kc)
            return run + jnp.sum(eqf, axis=0, keepdims=True)

        lax.fori_loop(0, nch, tie_body, jnp.zeros((1, tq), F32))

    thr = jnp.where(short, INT_MIN + 1, thr_ref[...])

    qs = [jnp.concatenate([q_ref[:, (g * GROUP_A + j) * HEAD_DIM:(g * GROUP_A + j + 1) * HEAD_DIM]
                           for j in range(GROUP_A)], axis=0) for g in range(N_KV_HEADS_A)]
    acc_ref[...] = jnp.zeros(acc_ref.shape, F32)

    def att_body(c, carry):
        s0 = pl.multiple_of(c * ch, ch)
        bias = jnp.where(keys_ref[pl.ds(s0, ch), :] >= thr, 0.0, -jnp.inf)
        bias = jnp.concatenate([bias] * GROUP_A, axis=1)
        out = []
        for g in range(N_KV_HEADS_A):
            m, l = carry[g]
            kc = k_ref[pl.ds(s0, ch), g * HEAD_DIM:(g + 1) * HEAD_DIM]
            lg = lax.dot_general(kc, qs[g], NT_DIMS, preferred_element_type=F32) + bias
            m_new = jnp.maximum(m, jnp.max(lg, axis=0, keepdims=True))
            m_safe = jnp.where(m_new == -jnp.inf, 0.0, m_new)
            p = jnp.exp2(lg - m_safe)
            alpha = jnp.exp2(m - m_safe)
            l = alpha * l + jnp.sum(p, axis=0, keepdims=True)
            pv = jnp.dot(vt_ref[0, c, g * HEAD_DIM:(g + 1) * HEAD_DIM, :], p.astype(BF16),
                         preferred_element_type=F32)
            acc_ref[g] = acc_ref[g] * alpha + pv
            out.append((m_new, l))
        return tuple(out)

    m0 = jnp.full((1, GROUP_A * tq), -jnp.inf, F32)
    l0 = jnp.zeros((1, GROUP_A * tq), F32)
    fin = lax.fori_loop(0, nch, att_body, ((m0, l0),) * N_KV_HEADS_A)
    for g in range(N_KV_HEADS_A):
        og = acc_ref[g] / fin[g][1]
        for j in range(GROUP_A):
            h = g * GROUP_A + j
            o_ref[:, h * HEAD_DIM:(h + 1) * HEAD_DIM] = og[:, j * tq:(j + 1) * tq].T.astype(o_ref.dtype)


def _dsa_attn(qit, wt, ki, q, k, vt, b, s, tq, ch, topk):
    nq = s // tq
    return pl.pallas_call(
        functools.partial(_dsa_attn_kernel, tq=tq, ch=ch, topk=topk),
        grid=(b, nq),
        in_specs=[pl.BlockSpec((1, 1, A_QI, tq), lambda bb, i: (bb, i, 0, 0)),
                  pl.BlockSpec((1, 1, N_IDX_HEADS, tq), lambda bb, i: (bb, i, 0, 0)),
                  pl.BlockSpec((s, IDX_DIM), lambda bb, i: (bb, 0)),
                  pl.BlockSpec((tq, A_Q), lambda bb, i: (bb * nq + i, 0)),
                  pl.BlockSpec((s, A_K), lambda bb, i: (bb, 0)),
                  pl.BlockSpec((1, s // ch, A_V, ch), lambda bb, i: (bb, 0, 0, 0))],
        out_specs=pl.BlockSpec((tq, A_Q), lambda bb, i: (bb * nq + i, 0)),
        out_shape=jax.ShapeDtypeStruct((b * s, A_Q), BF16),
        scratch_shapes=[pltpu.VMEM((s, tq), jnp.int32),
                        pltpu.VMEM((s, tq), jnp.int16),
                        pltpu.VMEM((s, tq), jnp.int16),
                        pltpu.VMEM((1, tq), jnp.int32),
                        pltpu.VMEM((1, tq), jnp.int32),
                        pltpu.VMEM((N_KV_HEADS_A, HEAD_DIM, GROUP_A * tq), F32)],
        compiler_params=_params(("parallel", "arbitrary")),
        name="dsa_attn",
    )(qit, wt, ki, q, k, vt)


def _sb_attn_kernel(q_ref, k_ref, vt_ref, o_ref, acc_ref, *, tq, hb):
    i = pl.program_id(2)
    ch = tq
    t_row = i * tq + lax.broadcasted_iota(jnp.int32, (1, tq), 1)
    r = lax.broadcasted_iota(jnp.int32, (ch, ch), 0)
    cc = lax.broadcasted_iota(jnp.int32, (ch, ch), 1)
    tri = jnp.where(cc >= r, 1.0, 0.0).astype(BF16)
    acc_ref[...] = jnp.zeros(acc_ref.shape, F32)

    def cond(st):
        j, _, low = st
        return jnp.logical_and(j <= i, low <= SB_DEAD_LOG2)

    def body(st):
        j, runs, _ = st
        c = i - j
        s0 = pl.multiple_of(c * ch, ch)
        strict = (s0 + lax.broadcasted_iota(jnp.int32, (ch, tq), 0)) < t_row
        new_runs = []
        for h in range(hb):
            hs = slice(h * HEAD_DIM, (h + 1) * HEAD_DIM)
            z = lax.dot_general(k_ref[pl.ds(s0, ch), hs], q_ref[:, hs], NT_DIMS,
                                preferred_element_type=F32)
            sp = jnp.maximum(z, 0.0) + jnp.log(1.0 + jnp.exp2(-jnp.abs(z))) * LOG2E
            sp = jnp.where(strict, sp, 0.0)
            hi = sp.astype(BF16)
            lo = (sp - hi.astype(F32)).astype(BF16)
            cum = (jnp.dot(tri, hi, preferred_element_type=F32)
                   + jnp.dot(tri, lo, preferred_element_type=F32) + runs[h])
            a = jnp.where(strict, jnp.exp2(z - cum), 0.0)
            acc_ref[h] += jnp.dot(vt_ref[0, c, hs, :], a.astype(BF16), preferred_element_type=F32)
            new_runs.append(runs[h] + jnp.sum(sp, axis=0, keepdims=True))
        low = jnp.min(functools.reduce(jnp.minimum, new_runs))
        return j + 1, tuple(new_runs), low

    run0 = tuple(jnp.zeros((1, tq), F32) for _ in range(hb))
    lax.while_loop(cond, body, (jnp.int32(0), run0, jnp.float32(0.0)))
    for h in range(hb):
        o_ref[:, h * HEAD_DIM:(h + 1) * HEAD_DIM] = acc_ref[h].T.astype(o_ref.dtype)


def _sb_attn(q, k, vt, b, s, tq, hb=4):
    nq = s // tq
    d = N_HEADS_B * HEAD_DIM
    w = hb * HEAD_DIM
    return pl.pallas_call(
        functools.partial(_sb_attn_kernel, tq=tq, hb=hb),
        grid=(b, N_HEADS_B // hb, nq),
        in_specs=[pl.BlockSpec((tq, w), lambda bb, h, i: (bb * nq + i, h)),
                  pl.BlockSpec((s, w), lambda bb, h, i: (bb, h)),
                  pl.BlockSpec((1, s // tq, w, tq), lambda bb, h, i: (bb, 0, h, 0))],
        out_specs=pl.BlockSpec((tq, w), lambda bb, h, i: (bb * nq + i, h)),
        out_shape=jax.ShapeDtypeStruct((b * s, d), BF16),
        scratch_shapes=[pltpu.VMEM((hb, HEAD_DIM, tq), F32)],
        compiler_params=_params(("parallel", "parallel", "arbitrary")),
        name="sb_attn",
    )(q, k, vt)


def _transpose_chunks_kernel(x_ref, o_ref):
    o_ref[0, 0] = x_ref[...].astype(F32).T.astype(o_ref.dtype)


def _transpose_chunks(x, d, col, b, s, ch):
    n = x.shape[0]
    nb = s // ch
    return pl.pallas_call(
        _transpose_chunks_kernel,
        grid=(n // ch,),
        in_specs=[pl.BlockSpec((ch, d), lambda i: (i, col))],
        out_specs=pl.BlockSpec((1, 1, d, ch), lambda i: (i // nb, i % nb, 0, 0)),
        out_shape=jax.ShapeDtypeStruct((b, nb, d, ch), x.dtype),
        compiler_params=_params(("parallel",)),
        name="transpose_chunks",
    )(x)


def _silu_mul(g, up):
    return g * (1.0 / (1.0 + jnp.exp(-g))) * up


def _ffn_kernel(x_ref, nw_ref, wg_ref, wu_ref, wd_ref, o_ref, u_ref):
    j = pl.program_id(1)

    @pl.when(j == 0)
    def _():
        x = x_ref[...]
        u_ref[...] = _rms(x, nw_ref[...]).astype(BF16)
        o_ref[...] = x

    u = u_ref[...]
    g = jnp.dot(u, wg_ref[...], preferred_element_type=F32)
    up = jnp.dot(u, wu_ref[...], preferred_element_type=F32)
    o_ref[...] += jnp.dot(_silu_mul(g, up).astype(BF16), wd_ref[...], preferred_element_type=F32)


def _ffn_dense(x, nw, wgu, wd, bm=512, fc=1408):
    n, d = x.shape
    nj = D_FF // fc
    return pl.pallas_call(
        _ffn_kernel,
        grid=(n // bm, nj),
        in_specs=[pl.BlockSpec((bm, d), lambda i, j: (i, 0)),
                  pl.BlockSpec((1, d), lambda i, j: (0, 0)),
                  pl.BlockSpec((d, fc), lambda i, j: (0, j)),
                  pl.BlockSpec((d, fc), lambda i, j: (0, j + nj)),
                  pl.BlockSpec((fc, d), lambda i, j: (j, 0))],
        out_specs=pl.BlockSpec((bm, d), lambda i, j: (i, 0)),
        out_shape=jax.ShapeDtypeStruct((n, d), F32),
        scratch_shapes=[pltpu.VMEM((bm, d), BF16)],
        compiler_params=_params(("parallel", "arbitrary")),
        name="ffn_dense",
    )(x, nw.reshape(1, d), wgu, wgu, wd)


def _router_kernel(x_ref, nw_ref, rwt_ref, g_ref, u_ref):
    u = _rms(x_ref[...], nw_ref[...])
    u_ref[...] = u.astype(BF16)
    lt = lax.dot_general(rwt_ref[...], u, NT_DIMS, preferred_element_type=F32,
                         precision=lax.Precision.HIGHEST)
    e = lax.broadcasted_iota(jnp.int32, lt.shape, 0)
    m1 = jnp.max(lt, axis=0, keepdims=True)
    i1 = jnp.min(jnp.where(lt == m1, e, N_EXPERTS), axis=0, keepdims=True)
    rest = jnp.where(e == i1, -jnp.inf, lt)
    m2 = jnp.max(rest, axis=0, keepdims=True)
    i2 = jnp.min(jnp.where(rest == m2, e, N_EXPERTS), axis=0, keepdims=True)
    w2 = jnp.exp(m2 - m1)
    den = 1.0 + w2
    g_ref[...] = jnp.where(e == i1, 1.0 / den, 0.0) + jnp.where(e == i2, w2 / den, 0.0)


def _router(x, nw, rwt, bm=512):
    n, d = x.shape
    return pl.pallas_call(
        _router_kernel,
        grid=(n // bm,),
        in_specs=[pl.BlockSpec((bm, d), lambda i: (i, 0)),
                  pl.BlockSpec((1, d), lambda i: (0, 0)),
                  pl.BlockSpec((N_EXPERTS, d), lambda i: (0, 0))],
        out_specs=(pl.BlockSpec((N_EXPERTS, bm), lambda i: (0, i)),
                   pl.BlockSpec((bm, d), lambda i: (i, 0))),
        out_shape=(jax.ShapeDtypeStruct((N_EXPERTS, n), F32),
                   jax.ShapeDtypeStruct((n, d), BF16)),
        compiler_params=_params(("parallel",)),
        name="moe_router",
    )(x, nw.reshape(1, d), rwt)


def _moe_rank_kernel(g_ref, rank_ref, carry_ref):
    @pl.when(pl.program_id(0) == 0)
    def _():
        carry_ref[...] = jnp.zeros(carry_ref.shape, F32)

    tc = g_ref.shape[1]
    m = jnp.where(g_ref[...] > 0.0, 1.0, 0.0)
    r = lax.broadcasted_iota(jnp.int32, (tc, tc), 0)
    c = lax.broadcasted_iota(jnp.int32, (tc, tc), 1)
    before = jnp.where(r < c, 1.0, 0.0).astype(BF16)
    rank_ref[...] = jnp.dot(m.astype(BF16), before, preferred_element_type=F32) + carry_ref[:, :1]
    carry_ref[...] = carry_ref[...] + jnp.sum(m, axis=1, keepdims=True)


def _moe_rank(gates, tc):
    e, n = gates.shape
    return pl.pallas_call(
        _moe_rank_kernel,
        grid=(n // tc,),
        in_specs=[pl.BlockSpec((e, tc), lambda i: (0, i))],
        out_specs=pl.BlockSpec((e, tc), lambda i: (0, i)),
        out_shape=jax.ShapeDtypeStruct((e, n), F32),
        scratch_shapes=[pltpu.VMEM((e, LANE), F32)],
        compiler_params=_params(("arbitrary",)),
        name="moe_rank",
    )(gates)


def _moe_plan(rank_all, gates, tc, tm):
    e_n, n = rank_all.shape
    nc = n // tc
    nt_max = 2 * n // tm + e_n
    ni_max = e_n * nc + nt_max
    i32 = jnp.int32
    cstart = rank_all[:, ::tc].astype(i32)
    counts = (rank_all[:, -1] + jnp.where(gates[:, -1] > 0.0, 1.0, 0.0)).astype(i32)
    cend = jnp.concatenate([cstart[:, 1:], counts[:, None]], axis=1)
    tiles = (counts + tm - 1) // tm
    tend = jnp.cumsum(tiles)
    tstart = tend - tiles
    total = tend[-1]
    t_idx = jnp.arange(nt_max, dtype=i32)
    tvalid = t_idx < total
    texp = jnp.sum((t_idx[:, None] >= tend[None, :]).astype(i32), axis=1)
    last_exp = jnp.sum((total - 1 >= tend).astype(i32))
    texp = jnp.where(tvalid, texp, last_exp)
    tsrc = jnp.where(tvalid, t_idx, 0)

    nonempty = cend > cstart
    t0 = cstart // tm
    t1 = (cend - 1) // tm
    local = jnp.stack([t0, t0 + 1], axis=-1)
    ok = jnp.stack([nonempty, nonempty & (t1 > t0)], axis=-1).reshape(-1)
    shape = local.shape
    tile = (tstart[:, None, None] + local).reshape(-1)
    base = (local * tm).reshape(-1)
    exp = jnp.broadcast_to(jnp.arange(e_n, dtype=i32)[:, None, None], shape).reshape(-1)
    chunk = jnp.broadcast_to(jnp.arange(nc, dtype=i32)[None, :, None], shape).reshape(-1)
    n_items = jnp.sum(ok.astype(i32))
    k_idx = jnp.arange(ni_max, dtype=i32)
    valid = k_idx < n_items

    def ordered(key, group):
        order = jnp.argsort(jnp.where(ok, key, jnp.int32(2 ** 30)))[:ni_max]
        src = jnp.where(valid, order, order[n_items - 1])
        grp = group[src]
        first = jnp.concatenate([jnp.ones((1,), bool), grp[1:] != grp[:-1]]) & valid
        return (tile[src], chunk[src], exp[src], base[src], first.astype(i32), valid.astype(i32))

    by_tile = ordered(tile * nc + chunk, tile)
    by_chunk = ordered(chunk * nt_max + tile, chunk)
    return (texp, tvalid.astype(i32), tsrc), by_tile, by_chunk, nt_max, ni_max


def _moe_gather_kernel(tile_s, chunk_s, exp_s, base_s, first_s, valid_s, u_ref, rank_ref, g_ref, x_ref, gs_ref):
    k = pl.program_id(0)
    tm, tc = x_ref.shape[0], u_ref.shape[0]

    @pl.when(valid_s[k] == 1)
    def _():
        e = exp_s[k]
        gate = g_ref[pl.ds(e, 1), :]
        rank = jnp.where(gate > 0.0, rank_ref[pl.ds(e, 1), :], -1.0)
        rows = (base_s[k] + lax.broadcasted_iota(jnp.int32, (tm, tc), 0)).astype(F32)
        p = jnp.where(rows == rank, 1.0, 0.0)
        xs = jnp.dot(p.astype(BF16), u_ref[...], preferred_element_type=F32).astype(BF16)
        gsel = jnp.sum(p * gate, axis=1, keepdims=True)

        @pl.when(first_s[k] == 1)
        def _():
            x_ref[...] = xs
            gs_ref[...] = gsel

        @pl.when(first_s[k] == 0)
        def _():
            x_ref[...] = x_ref[...] + xs
            gs_ref[...] = gs_ref[...] + gsel


def _moe_gather(u, rank_all, gates, items, nt_max, ni_max, tc, tm):
    n, d = u.shape
    e = gates.shape[0]
    spec = pltpu.PrefetchScalarGridSpec(
        num_scalar_prefetch=6,
        grid=(ni_max,),
        in_specs=[pl.BlockSpec((tc, d), lambda k, t, c, *_: (c[k], 0)),
                  pl.BlockSpec((e, tc), lambda k, t, c, *_: (0, c[k])),
                  pl.BlockSpec((e, tc), lambda k, t, c, *_: (0, c[k]))],
        out_specs=(pl.BlockSpec((tm, d), lambda k, t, c, *_: (t[k], 0)),
                   pl.BlockSpec((tm, 1), lambda k, t, c, *_: (t[k], 0))),
    )
    return pl.pallas_call(
        _moe_gather_kernel,
        grid_spec=spec,
        out_shape=(jax.ShapeDtypeStruct((nt_max * tm, d), BF16),
                   jax.ShapeDtypeStruct((nt_max * tm, 1), F32)),
        compiler_params=_params(("arbitrary",)),
        name="moe_gather",
    )(*items, u, rank_all, gates)


def _moe_ffn_kernel(texp_s, tvalid_s, tsrc_s, x_ref, gs_ref, wg_ref, wu_ref, wd_ref, o_ref, acc_ref):
    t = pl.program_id(0)
    j = pl.program_id(1)
    valid = tvalid_s[t] == 1

    @pl.when(valid)
    def _():
        x = x_ref[...]
        g = jnp.dot(x, wg_ref[0], preferred_element_type=F32)
        up = jnp.dot(x, wu_ref[0], preferred_element_type=F32)
        hm = (_silu_mul(g, up) * gs_ref[...]).astype(BF16)
        y = jnp.dot(hm, wd_ref[0], preferred_element_type=F32)

        @pl.when(j == 0)
        def _():
            acc_ref[...] = y

        @pl.when(j > 0)
        def _():
            acc_ref[...] += y

    @pl.when(j == pl.num_programs(1) - 1)
    def _():
        @pl.when(valid)
        def _():
            o_ref[...] = acc_ref[...].astype(o_ref.dtype)

        @pl.when(jnp.logical_not(valid))
        def _():
            o_ref[...] = jnp.zeros(o_ref.shape, o_ref.dtype)


def _moe_ffn(xs, gs, tiles, wgu, wd, nt_max, tm, fc=1408):
    d = xs.shape[1]
    nj = D_FF // fc
    spec = pltpu.PrefetchScalarGridSpec(
        num_scalar_prefetch=3,
        grid=(nt_max, nj),
        in_specs=[pl.BlockSpec((tm, d), lambda t, j, te, tv, ts: (ts[t], 0)),
                  pl.BlockSpec((tm, 1), lambda t, j, te, tv, ts: (ts[t], 0)),
                  pl.BlockSpec((1, d, fc), lambda t, j, te, tv, ts: (te[t], 0, j)),
                  pl.BlockSpec((1, d, fc), lambda t, j, te, tv, ts: (te[t], 0, j + nj)),
                  pl.BlockSpec((1, fc, d), lambda t, j, te, tv, ts: (te[t], j, 0))],
        out_specs=pl.BlockSpec((tm, d), lambda t, j, te, tv, ts: (t, 0)),
        scratch_shapes=[pltpu.VMEM((tm, d), F32)],
    )
    return pl.pallas_call(
        _moe_ffn_kernel,
        grid_spec=spec,
        out_shape=jax.ShapeDtypeStruct((nt_max * tm, d), BF16),
        compiler_params=_params(("arbitrary", "arbitrary")),
        name="moe_ffn",
    )(*tiles, xs, gs, wgu, wgu, wd)


def _moe_combine_kernel(tile_s, chunk_s, exp_s, base_s, first_s, valid_s, h_ref, rcol_ref, ys_ref, o_ref):
    k = pl.program_id(0)
    tc, tm = h_ref.shape[0], ys_ref.shape[0]

    @pl.when(first_s[k] == 1)
    def _():
        o_ref[...] = h_ref[...]

    @pl.when(valid_s[k] == 1)
    def _():
        lane = lax.broadcasted_iota(jnp.int32, rcol_ref.shape, 1)
        rank = jnp.sum(jnp.where(lane == exp_s[k], rcol_ref[...], 0.0), axis=1, keepdims=True)
        cols = (base_s[k] + lax.broadcasted_iota(jnp.int32, (tc, tm), 1)).astype(F32)
        p = jnp.where(cols == rank, 1.0, 0.0).astype(BF16)
        o_ref[...] += jnp.dot(p, ys_ref[...], preferred_element_type=F32)


def _moe_combine(h, rank_col, ys, items, ni_max, tc, tm):
    n, d = h.shape
    e = rank_col.shape[1]
    spec = pltpu.PrefetchScalarGridSpec(
        num_scalar_prefetch=6,
        grid=(ni_max,),
        in_specs=[pl.BlockSpec((tc, d), lambda k, t, c, *_: (c[k], 0)),
                  pl.BlockSpec((tc, e), lambda k, t, c, *_: (c[k], 0)),
                  pl.BlockSpec((tm, d), lambda k, t, c, *_: (t[k], 0))],
        out_specs=pl.BlockSpec((tc, d), lambda k, t, c, *_: (c[k], 0)),
    )
    return pl.pallas_call(
        _moe_combine_kernel,
        grid_spec=spec,
        out_shape=jax.ShapeDtypeStruct((n, d), F32),
        compiler_params=_params(("arbitrary",)),
        name="moe_combine",
    )(*items, h, rank_col, ys)


def _moe_grouped(h, nw, rwt, wgu, wd, tc=512, tm=512):
    gates, u = _router(h, nw, rwt)
    rank_all = _moe_rank(gates, tc)
    tiles, by_tile, by_chunk, nt_max, ni_max = _moe_plan(rank_all, gates, tc, tm)
    xs, gs = _moe_gather(u, rank_all, gates, by_tile, nt_max, ni_max, tc, tm)
    ys = _moe_ffn(xs, gs, tiles, wgu, wd, nt_max, tm)
    rank_col = jnp.where(gates > 0.0, rank_all, -1.0).T
    return _moe_combine(h, rank_col, ys, by_chunk, ni_max, tc, tm)


def kernel(x, a_norm_w, a_w_in, a_q_norm_w, a_k_norm_w, a_w_out, kv_norm_w, kv_w, b_norm_w, b_w_q, b_w_out,
           ffn_norm_w, dense_w_gate_up, dense_w_down, moe_router_w, moe_w_gate_up, moe_w_down):
    b, s, d = x.shape
    n = b * s
    topk = min(MAX_TOPK, s // 4)
    tq_a, ch_a, tq_b = 256, 256, 256
    h = x.reshape(n, d)

    w_in = jnp.pad(a_w_in[0], ((0, 0), (0, A_IN_PAD - A_IN))).astype(BF16)
    proj = _norm_matmul(h, a_norm_w[0], w_in, F32)
    q, k, vt, qit, ki, wt = _dsa_prep(proj, a_q_norm_w[0], a_k_norm_w[0], b, s, tq_a, ch_a)
    o = _dsa_attn(qit, wt, ki, q, k, vt, b, s, tq_a, ch_a, topk)
    h = _matmul_residual(o, a_w_out[0].astype(BF16), h)
    h = _ffn_dense(h, ffn_norm_w[0], dense_w_gate_up[0].astype(BF16), dense_w_down[0].astype(BF16))

    kv = _norm_matmul(h, kv_norm_w, kv_w.astype(BF16), BF16)
    vt_sb = _transpose_chunks(kv, d, 1, b, s, tq_b)

    q_sb = _norm_matmul(h, b_norm_w[0], (b_w_q[0] * (HEAD_DIM ** -0.5 * LOG2E)).astype(BF16), BF16)
    o = _sb_attn(q_sb, kv, vt_sb, b, s, tq_b)
    h = _matmul_residual(o, b_w_out[0].astype(BF16), h)
    h = _moe_grouped(h, ffn_norm_w[1], moe_router_w[0].T,
                     moe_w_gate_up[0].astype(BF16), moe_w_down[0].astype(BF16))
    return h.reshape(b, s, d)
```

```python
import functools

import jax
import jax.numpy as jnp
from jax import lax
from jax.experimental import pallas as pl
from jax.experimental.pallas import tpu as pltpu

D_MODEL = 1024
HEAD_DIM = 128
N_HEADS_A = 8
N_KV_HEADS_A = 2
GROUP_A = 4
N_IDX_HEADS = 8
IDX_DIM = 64
MAX_TOPK = 256
N_HEADS_B = 8
ROPE_THETA = 500000.0
ROT_FRAC = 4
NORM_EPS = 1e-6
D_FF = 2816
N_EXPERTS = 8

A_Q = N_HEADS_A * HEAD_DIM
A_K = N_KV_HEADS_A * HEAD_DIM
A_V = N_KV_HEADS_A * HEAD_DIM
A_QI = N_IDX_HEADS * IDX_DIM
A_IN = A_Q + A_K + A_V + A_QI + IDX_DIM + N_IDX_HEADS
A_IN_PAD = A_Q + A_K + A_V + A_QI + 128

LANE = 128
INT_MIN = -(2 ** 31)
HALF_MIN = -(2 ** 15)
HALF_ROWS = 16
VT_ROWS = HEAD_DIM + HALF_ROWS
LOG2E = 1.4426950408889634
SB_DEAD_LOG2 = 151.0
VMEM_LIMIT = 56 * 1024 * 1024

F32 = jnp.float32
BF16 = jnp.bfloat16
NT_DIMS = (((1,), (1,)), ((), ()))


def _params(sem):
    return pltpu.CompilerParams(dimension_semantics=sem, vmem_limit_bytes=VMEM_LIMIT)


def _rms(x, w):
    ms = jnp.mean(x * x, axis=-1, keepdims=True)
    return x * lax.rsqrt(ms + NORM_EPS) * w


def _norm_matmul_kernel(x_ref, nw_ref, w_ref, o_ref):
    u = _rms(x_ref[...], nw_ref[...]).astype(BF16)
    o_ref[...] = jnp.dot(u, w_ref[...], preferred_element_type=F32).astype(o_ref.dtype)


def _norm_matmul(x, nw, w, out_dtype, bm=512):
    n, d = x.shape
    m = w.shape[1]
    return pl.pallas_call(
        _norm_matmul_kernel,
        grid=(n // bm,),
        in_specs=[pl.BlockSpec((bm, d), lambda i: (i, 0)),
                  pl.BlockSpec((1, d), lambda i: (0, 0)),
                  pl.BlockSpec((d, m), lambda i: (0, 0))],
        out_specs=pl.BlockSpec((bm, m), lambda i: (i, 0)),
        out_shape=jax.ShapeDtypeStruct((n, m), out_dtype),
        compiler_params=_params(("parallel",)),
        name="norm_matmul",
    )(x, nw.reshape(1, d), w)


def _matmul_res_kernel(a_ref, w_ref, r_ref, o_ref):
    o_ref[...] = r_ref[...] + jnp.dot(a_ref[...], w_ref[...], preferred_element_type=F32)


def _matmul_residual(a, w, res, bm=512):
    n, k = a.shape
    m = w.shape[1]
    return pl.pallas_call(
        _matmul_res_kernel,
        grid=(n // bm,),
        in_specs=[pl.BlockSpec((bm, k), lambda i: (i, 0)),
                  pl.BlockSpec((k, m), lambda i: (0, 0)),
                  pl.BlockSpec((bm, m), lambda i: (i, 0))],
        out_specs=pl.BlockSpec((bm, m), lambda i: (i, 0)),
        out_shape=jax.ShapeDtypeStruct((n, m), F32),
        compiler_params=_params(("parallel",)),
        name="matmul_residual",
    )(a, w, res)


def _rope(x, c, s_lo, s_hi, half):
    n = x.shape[-1]
    return x * c + pltpu.roll(x, n - half, 1) * s_lo + pltpu.roll(x, half, 1) * s_hi


def _dsa_prep_kernel(p_ref, qw_ref, kw_ref, c_ref, slo_ref, shi_ref, ci_ref, sloi_ref, shii_ref,
                     q_ref, k_ref, vt_ref, qit_ref, ki_ref, wt_ref, *, tq):
    bm = p_ref.shape[0]
    c, slo, shi = c_ref[...], slo_ref[...], shi_ref[...]
    ci, sloi, shii = ci_ref[...], sloi_ref[...], shii_ref[...]
    scale = HEAD_DIM ** -0.5 * LOG2E
    for h in range(N_HEADS_A):
        xh = _rms(p_ref[:, h * HEAD_DIM:(h + 1) * HEAD_DIM], qw_ref[...])
        q_ref[:, h * HEAD_DIM:(h + 1) * HEAD_DIM] = (_rope(xh, c, slo, shi, 16) * scale).astype(BF16)
    for h in range(N_KV_HEADS_A):
        xh = _rms(p_ref[:, A_Q + h * HEAD_DIM:A_Q + (h + 1) * HEAD_DIM], kw_ref[...])
        k_ref[:, h * HEAD_DIM:(h + 1) * HEAD_DIM] = _rope(xh, c, slo, shi, 16).astype(BF16)
    ones = jnp.ones((VT_ROWS - HEAD_DIM, bm), BF16)
    for h in range(N_KV_HEADS_A):
        v = p_ref[:, A_Q + A_K + h * HEAD_DIM:A_Q + A_K + (h + 1) * HEAD_DIM]
        vt_ref[0, 0, h * VT_ROWS:h * VT_ROWS + HEAD_DIM, :] = v.T.astype(BF16)
        vt_ref[0, 0, h * VT_ROWS + HEAD_DIM:(h + 1) * VT_ROWS, :] = ones
    off = A_Q + A_K + A_V
    for j in range(A_QI // LANE):
        xj = _rope(p_ref[:, off + j * LANE:off + (j + 1) * LANE], ci, sloi, shii, 8)
        xt = xj.T.astype(BF16)
        for b in range(bm // tq):
            qit_ref[0, b, j * LANE:(j + 1) * LANE, :] = xt[:, b * tq:(b + 1) * tq]
    off += A_QI
    lane = lax.broadcasted_iota(jnp.int32, (1, LANE), 1)
    first = lane < IDX_DIM
    kw = p_ref[:, off:off + LANE]
    kw = _rope(kw, jnp.where(first, ci, 1.0), jnp.where(first, sloi, 0.0), jnp.where(first, shii, 0.0), 8)
    ki_ref[...] = kw[:, :IDX_DIM].astype(BF16)
    wt = kw.T[IDX_DIM:IDX_DIM + N_IDX_HEADS, :] * (N_IDX_HEADS ** -0.5) * (IDX_DIM ** -0.5)
    for b in range(bm // tq):
        wt_ref[0, b] = wt[:, b * tq:(b + 1) * tq]


def _rope_tables(s, head, reps):
    r = head // ROT_FRAC
    half = r // 2
    inv = ROPE_THETA ** (-jnp.arange(half, dtype=F32) * 2.0 / r)
    ang = jnp.arange(s).astype(F32)[:, None] * inv[None, :]
    cos, sin = jnp.cos(ang), jnp.sin(ang)
    one = jnp.ones((s, head - r), F32)
    zero = jnp.zeros((s, head - r), F32)
    zh = jnp.zeros((s, half), F32)
    c = jnp.concatenate([cos, cos, one], axis=1)
    s_lo = jnp.concatenate([-sin, zh, zero], axis=1)
    s_hi = jnp.concatenate([zh, sin, zero], axis=1)
    return tuple(jnp.tile(t, (1, reps)) for t in (c, s_lo, s_hi))


def _dsa_prep(proj, qw, kw, b, s, tq, ch):
    n = proj.shape[0]
    bm = ch
    nb = s // bm
    c, slo, shi = _rope_tables(s, HEAD_DIM, 1)
    ci, sloi, shii = _rope_tables(s, IDX_DIM, 2)
    tab = pl.BlockSpec((bm, LANE), lambda i: (i % nb, 0))
    row = lambda w: pl.BlockSpec((bm, w), lambda i: (i, 0))
    out_shapes = (
        jax.ShapeDtypeStruct((n, A_Q), BF16),
        jax.ShapeDtypeStruct((n, A_K), BF16),
        jax.ShapeDtypeStruct((b, s // ch, N_KV_HEADS_A * VT_ROWS, ch), BF16),
        jax.ShapeDtypeStruct((b, s // tq, A_QI, tq), BF16),
        jax.ShapeDtypeStruct((n, IDX_DIM), BF16),
        jax.ShapeDtypeStruct((b, s // tq, N_IDX_HEADS, tq), F32),
    )
    out_specs = (
        row(A_Q), row(A_K),
        pl.BlockSpec((1, 1, N_KV_HEADS_A * VT_ROWS, ch), lambda i: (i // nb, i % nb, 0, 0)),
        pl.BlockSpec((1, bm // tq, A_QI, tq), lambda i: (i // nb, i % nb, 0, 0)),
        row(IDX_DIM),
        pl.BlockSpec((1, bm // tq, N_IDX_HEADS, tq), lambda i: (i // nb, i % nb, 0, 0)),
    )
    return pl.pallas_call(
        functools.partial(_dsa_prep_kernel, tq=tq),
        grid=(n // bm,),
        in_specs=[row(A_IN_PAD),
                  pl.BlockSpec((1, HEAD_DIM), lambda i: (0, 0)),
                  pl.BlockSpec((1, HEAD_DIM), lambda i: (0, 0)),
                  tab, tab, tab, tab, tab, tab],
        out_specs=out_specs,
        out_shape=out_shapes,
        compiler_params=_params(("parallel",)),
        name="dsa_prep",
    )(proj, qw.reshape(1, -1), kw.reshape(1, -1), c, slo, shi, ci, sloi, shii)


def _dsa_attn_kernel(qit_ref, wt_ref, ki_ref, q_ref, k_ref, vt_ref, o_ref,
                     keys_ref, hi_ref, lo_ref, thr_ref, cnt_ref, acc_ref, *, tq, ch, topk):
    i = pl.program_id(1)
    nch = ((i + 1) * tq + ch - 1) // ch
    t_row = i * tq + lax.broadcasted_iota(jnp.int32, (1, tq), 1)

    qit = jnp.concatenate([qit_ref[0, 0, h * IDX_DIM:(h + 1) * IDX_DIM, :] for h in range(N_IDX_HEADS)],
                          axis=1)
    wt = wt_ref[0, 0]

    npair = (nch + 1) // 2

    def score_body(c2, carry):
        starts = [pl.multiple_of(c2 * (2 * ch) + half * ch, ch) for half in range(2)]
        rels = [jnp.dot(ki_ref[pl.ds(s0, ch), :], qit, preferred_element_type=F32) for s0 in starts]
        for s0, rel in zip(starts, rels):
            sc = jnp.zeros((ch, tq), F32)
            for h in range(N_IDX_HEADS):
                sc = sc + jnp.maximum(rel[:, h * tq:(h + 1) * tq], 0.0) * wt[h:h + 1, :]
            sc = jnp.where(sc == 0.0, 0.0, sc)
            bits = lax.bitcast_convert_type(sc, jnp.int32)
            key = bits ^ ((bits >> 31) & 0x7FFFFFFF)
            s_idx = s0 + lax.broadcasted_iota(jnp.int32, (ch, tq), 0)
            key = jnp.where(s_idx <= t_row, key, INT_MIN)
            keys_ref[pl.ds(s0, ch), :] = key
            hi_ref[pl.ds(s0, ch), :] = (key >> 16).astype(jnp.int16)
        return carry

    lax.fori_loop(0, npair, score_body, 0)

    def count16(ref, thr, strict):
        t16 = jnp.broadcast_to(thr.astype(jnp.int16), (HALF_ROWS, tq))

        def body(c2, acc):
            s0 = pl.multiple_of(c2 * (2 * ch), 2 * ch)
            x = ref[pl.ds(s0, 2 * ch), :].reshape(2 * ch // HALF_ROWS, HALF_ROWS, tq)
            hit = jnp.where((x > t16[None]) if strict else (x >= t16[None]), jnp.int16(1), jnp.int16(0))
            parts = [hit[r] for r in range(2 * ch // HALF_ROWS)]
            while len(parts) > 1:
                parts = [a + b for a, b in zip(parts[0::2], parts[1::2])]
            return acc + parts[0]

        acc = lax.fori_loop(0, npair, body, jnp.zeros((HALF_ROWS, tq), jnp.int16))
        return jnp.sum(acc.astype(jnp.int32), axis=0, keepdims=True)

    def bisect_pass(ref, want, bit, st):
        thr, cnt = st
        cand = thr + (jnp.int32(1) << bit)
        c = count16(ref, cand, False)
        take = c >= want
        return jnp.where(take, cand, thr), jnp.where(take, c, cnt)

    def bisect_start(ref, want, cnt_floor):
        c0 = count16(ref, jnp.zeros((1, tq), jnp.int32), False)
        pos = c0 >= want
        return jnp.where(pos, 0, HALF_MIN), jnp.where(pos, c0, cnt_floor)

    short = t_row < topk
    want_hi = jnp.full((1, tq), topk, jnp.int32)
    st = bisect_start(hi_ref, want_hi, jnp.full((1, tq), npair * 2 * ch, jnp.int32))
    thr_hi, cnt_hi = lax.fori_loop(0, 15, lambda it, s: bisect_pass(hi_ref, want_hi, 14 - it, s), st)
    above = count16(hi_ref, thr_hi, True)
    want_lo = topk - above

    def lo_body(c2, carry):
        s0 = pl.multiple_of(c2 * (2 * ch), 2 * ch)
        key = keys_ref[pl.ds(s0, 2 * ch), :]
        lo = (key & 0xFFFF) + HALF_MIN
        lo_ref[pl.ds(s0, 2 * ch), :] = jnp.where((key >> 16) == thr_hi, lo, HALF_MIN).astype(jnp.int16)
        return carry

    lax.fori_loop(0, npair, lo_body, 0)

    def n_open(cnt):
        return jnp.sum(jnp.where(jnp.logical_and(jnp.logical_not(short), cnt != want_lo), 1, 0))

    def lo_cond(s):
        it, _, _, n = s
        return jnp.logical_and(it < 15, n > 0)

    def lo_step(s):
        it, thr, cnt, _ = s
        for k in range(3):
            thr, cnt = bisect_pass(lo_ref, want_lo, 14 - it - k, (thr, cnt))
        return it + 3, thr, cnt, n_open(cnt)

    thr_lo, cnt_lo = bisect_start(lo_ref, want_lo, cnt_hi - above)
    _, thr_lo, cnt_lo, left = lax.while_loop(lo_cond, lo_step, (jnp.int32(0), thr_lo, cnt_lo, n_open(cnt_lo)))
    thr_ref[...] = thr_hi * 65536 + (thr_lo - HALF_MIN)
    cnt_ref[...] = above + cnt_lo

    def open_rows():
        return jnp.logical_and(jnp.logical_not(short), cnt_ref[...] != topk)

    @pl.when(left > 0)
    def _():
        thr = jnp.where(open_rows(), thr_ref[...], INT_MIN)

        def gt_body(c, acc):
            s0 = pl.multiple_of(c * ch, ch)
            hit = jnp.where(keys_ref[pl.ds(s0, ch), :] > thr, 1, 0)
            return acc + jnp.sum(hit.reshape(ch // 8, 8, tq), axis=0)
        gt = jnp.sum(lax.fori_loop(0, nch, gt_body, jnp.zeros((8, tq), jnp.int32)), axis=0, keepdims=True)
        need = (topk - gt).astype(F32)
        r = lax.broadcasted_iota(jnp.int32, (ch, ch), 0)
        cc = lax.broadcasted_iota(jnp.int32, (ch, ch), 1)
        tri = jnp.where(cc <= r, 1.0, 0.0).astype(BF16)

        def tie_body(c, run):
            s0 = pl.multiple_of(c * ch, ch)
            kc = keys_ref[pl.ds(s0, ch), :]
            eq = kc == thr
            eqf = jnp.where(eq, 1.0, 0.0)
            pre = jnp.dot(tri, eqf.astype(BF16), preferred_element_type=F32) + run
            drop = jnp.logical_and(eq, pre > need)
            keys_ref[pl.ds(s0, ch), :] = jnp.where(drop, INT_MIN, kc)
            return run + jnp.sum(eqf, axis=0, keepdims=True)

        lax.fori_loop(0, nch, tie_body, jnp.zeros((1, tq), F32))

    thr = jnp.where(short, INT_MIN + 1, thr_ref[...])

    heads = range(N_HEADS_A)
    acc_ref[...] = jnp.zeros(acc_ref.shape, F32)

    def att_body(c, carry):
        s0 = pl.multiple_of(c * ch, ch)
        bias = jnp.where(keys_ref[pl.ds(s0, ch), :] >= thr, 0.0, -jnp.inf)
        lgs = [lax.dot_general(k_ref[pl.ds(s0, ch), (h // GROUP_A) * HEAD_DIM:(h // GROUP_A + 1) * HEAD_DIM],
                               q_ref[:, h * HEAD_DIM:(h + 1) * HEAD_DIM], NT_DIMS,
                               preferred_element_type=F32) for h in heads]
        lgs = [(lg + bias).astype(BF16) for lg in lgs]
        m_new = [jnp.maximum(carry[h], jnp.max(lgs[h], axis=0, keepdims=True).astype(F32)) for h in heads]
        m_safe = [jnp.where(m == -jnp.inf, 0.0, m) for m in m_new]
        ps = [jnp.exp2(lgs[h] - m_safe[h].astype(BF16)) for h in heads]
        pvs = [jnp.dot(vt_ref[0, c, (h // GROUP_A) * VT_ROWS:(h // GROUP_A + 1) * VT_ROWS, :], ps[h],
                       preferred_element_type=F32) for h in heads]
        for h in heads:
            acc_ref[h] = acc_ref[h] * jnp.exp2(carry[h] - m_safe[h]) + pvs[h]
        return tuple(m_new)

    m0 = jnp.full((1, tq), -jnp.inf, F32)
    lax.fori_loop(0, nch, att_body, (m0,) * N_HEADS_A)
    for h in heads:
        og = acc_ref[h, :HEAD_DIM, :] / acc_ref[h, HEAD_DIM:HEAD_DIM + 1, :]
        o_ref[:, h * HEAD_DIM:(h + 1) * HEAD_DIM] = og.T.astype(o_ref.dtype)


def _dsa_attn(qit, wt, ki, q, k, vt, b, s, tq, ch, topk):
    nq = s // tq
    return pl.pallas_call(
        functools.partial(_dsa_attn_kernel, tq=tq, ch=ch, topk=topk),
        grid=(b, nq),
        in_specs=[pl.BlockSpec((1, 1, A_QI, tq), lambda bb, i: (bb, i, 0, 0)),
                  pl.BlockSpec((1, 1, N_IDX_HEADS, tq), lambda bb, i: (bb, i, 0, 0)),
                  pl.BlockSpec((s, IDX_DIM), lambda bb, i: (bb, 0)),
                  pl.BlockSpec((tq, A_Q), lambda bb, i: (bb * nq + i, 0)),
                  pl.BlockSpec((s, A_K), lambda bb, i: (bb, 0)),
                  pl.BlockSpec((1, s // ch, N_KV_HEADS_A * VT_ROWS, ch), lambda bb, i: (bb, 0, 0, 0))],
        out_specs=pl.BlockSpec((tq, A_Q), lambda bb, i: (bb * nq + i, 0)),
        out_shape=jax.ShapeDtypeStruct((b * s, A_Q), BF16),
        scratch_shapes=[pltpu.VMEM((s, tq), jnp.int32),
                        pltpu.VMEM((s, tq), jnp.int16),
                        pltpu.VMEM((s, tq), jnp.int16),
                        pltpu.VMEM((1, tq), jnp.int32),
                        pltpu.VMEM((1, tq), jnp.int32),
                        pltpu.VMEM((N_HEADS_A, VT_ROWS, tq), F32)],
        compiler_params=_params(("parallel", "arbitrary")),
        name="dsa_attn",
    )(qit, wt, ki, q, k, vt)


def _sb_attn_kernel(q_ref, k_ref, vt_ref, o_ref, acc_ref, *, tq, hb):
    i = pl.program_id(2)
    ch = tq
    t_row = i * tq + lax.broadcasted_iota(jnp.int32, (1, tq), 1)
    r = lax.broadcasted_iota(jnp.int32, (ch, ch), 0)
    cc = lax.broadcasted_iota(jnp.int32, (ch, ch), 1)
    tri = jnp.where(cc >= r, 1.0, 0.0).astype(BF16)
    acc_ref[...] = jnp.zeros(acc_ref.shape, F32)

    def cond(st):
        j, _, low = st
        return jnp.logical_and(j <= i, low <= SB_DEAD_LOG2)

    def body(st):
        j, runs, _ = st
        c = i - j
        s0 = pl.multiple_of(c * ch, ch)
        strict = (s0 + lax.broadcasted_iota(jnp.int32, (ch, tq), 0)) < t_row
        heads = [slice(h * HEAD_DIM, (h + 1) * HEAD_DIM) for h in range(hb)]
        zs = [lax.dot_general(k_ref[pl.ds(s0, ch), hs], q_ref[:, hs], NT_DIMS, preferred_element_type=F32)
              for hs in heads]
        sps = [jnp.where(strict, jnp.maximum(z, 0.0) + jnp.log(1.0 + jnp.exp2(-jnp.abs(z))) * LOG2E, 0.0)
               for z in zs]
        his = [sp.astype(BF16) for sp in sps]
        los = [(sp - hi.astype(F32)).astype(BF16) for sp, hi in zip(sps, his)]
        cums = [jnp.dot(tri, hi, preferred_element_type=F32) + jnp.dot(tri, lo, preferred_element_type=F32) + run
                for hi, lo, run in zip(his, los, runs)]
        aas = [jnp.where(strict, jnp.exp2(z - cum), 0.0).astype(BF16) for z, cum in zip(zs, cums)]
        pvs = [jnp.dot(vt_ref[0, c, hs, :], a, preferred_element_type=F32) for hs, a in zip(heads, aas)]
        new_runs = [run + jnp.sum(sp, axis=0, keepdims=True) for run, sp in zip(runs, sps)]
        for h in range(hb):
            acc_ref[h] += pvs[h]
        low = jnp.min(functools.reduce(jnp.minimum, new_runs))
        return j + 1, tuple(new_runs), low

    run0 = tuple(jnp.zeros((1, tq), F32) for _ in range(hb))
    lax.while_loop(cond, body, (jnp.int32(0), run0, jnp.float32(0.0)))
    for h in range(hb):
        o_ref[:, h * HEAD_DIM:(h + 1) * HEAD_DIM] = acc_ref[h].T.astype(o_ref.dtype)


def _sb_attn(q, k, vt, b, s, tq, hb=4):
    nq = s // tq
    d = N_HEADS_B * HEAD_DIM
    w = hb * HEAD_DIM
    return pl.pallas_call(
        functools.partial(_sb_attn_kernel, tq=tq, hb=hb),
        grid=(b, N_HEADS_B // hb, nq),
        in_specs=[pl.BlockSpec((tq, w), lambda bb, h, i: (bb * nq + i, h)),
                  pl.BlockSpec((s, w), lambda bb, h, i: (bb, h)),
                  pl.BlockSpec((1, s // tq, w, tq), lambda bb, h, i: (bb, 0, h, 0))],
        out_specs=pl.BlockSpec((tq, w), lambda bb, h, i: (bb * nq + i, h)),
        out_shape=jax.ShapeDtypeStruct((b * s, d), BF16),
        scratch_shapes=[pltpu.VMEM((hb, HEAD_DIM, tq), F32)],
        compiler_params=_params(("parallel", "parallel", "arbitrary")),
        name="sb_attn",
    )(q, k, vt)


def _transpose_chunks_kernel(x_ref, o_ref):
    o_ref[0, 0] = x_ref[...].astype(F32).T.astype(o_ref.dtype)


def _transpose_chunks(x, d, col, b, s, ch):
    n = x.shape[0]
    nb = s // ch
    return pl.pallas_call(
        _transpose_chunks_kernel,
        grid=(n // ch,),
        in_specs=[pl.BlockSpec((ch, d), lambda i: (i, col))],
        out_specs=pl.BlockSpec((1, 1, d, ch), lambda i: (i // nb, i % nb, 0, 0)),
        out_shape=jax.ShapeDtypeStruct((b, nb, d, ch), x.dtype),
        compiler_params=_params(("parallel",)),
        name="transpose_chunks",
    )(x)


def _silu_mul(g, up):
    return g * (1.0 / (1.0 + jnp.exp(-g))) * up


def _ffn_kernel(x_ref, nw_ref, wg_ref, wu_ref, wd_ref, o_ref, u_ref):
    j = pl.program_id(1)

    @pl.when(j == 0)
    def _():
        x = x_ref[...]
        u_ref[...] = _rms(x, nw_ref[...]).astype(BF16)
        o_ref[...] = x

    u = u_ref[...]
    g = jnp.dot(u, wg_ref[...], preferred_element_type=F32)
    up = jnp.dot(u, wu_ref[...], preferred_element_type=F32)
    o_ref[...] += jnp.dot(_silu_mul(g, up).astype(BF16), wd_ref[...], preferred_element_type=F32)


def _ffn_dense(x, nw, wgu, wd, bm=512, fc=1408):
    n, d = x.shape
    nj = D_FF // fc
    return pl.pallas_call(
        _ffn_kernel,
        grid=(n // bm, nj),
        in_specs=[pl.BlockSpec((bm, d), lambda i, j: (i, 0)),
                  pl.BlockSpec((1, d), lambda i, j: (0, 0)),
                  pl.BlockSpec((d, fc), lambda i, j: (0, j)),
                  pl.BlockSpec((d, fc), lambda i, j: (0, j + nj)),
                  pl.BlockSpec((fc, d), lambda i, j: (j, 0))],
        out_specs=pl.BlockSpec((bm, d), lambda i, j: (i, 0)),
        out_shape=jax.ShapeDtypeStruct((n, d), F32),
        scratch_shapes=[pltpu.VMEM((bm, d), BF16)],
        compiler_params=_params(("parallel", "arbitrary")),
        name="ffn_dense",
    )(x, nw.reshape(1, d), wgu, wgu, wd)


def _router_kernel(x_ref, nw_ref, rwt_ref, g_ref, u_ref):
    u = _rms(x_ref[...], nw_ref[...])
    u_ref[...] = u.astype(BF16)
    lt = lax.dot_general(rwt_ref[...], u, NT_DIMS, preferred_element_type=F32,
                         precision=lax.Precision.HIGHEST)
    e = lax.broadcasted_iota(jnp.int32, lt.shape, 0)
    m1 = jnp.max(lt, axis=0, keepdims=True)
    i1 = jnp.min(jnp.where(lt == m1, e, N_EXPERTS), axis=0, keepdims=True)
    rest = jnp.where(e == i1, -jnp.inf, lt)
    m2 = jnp.max(rest, axis=0, keepdims=True)
    i2 = jnp.min(jnp.where(rest == m2, e, N_EXPERTS), axis=0, keepdims=True)
    w2 = jnp.exp(m2 - m1)
    den = 1.0 + w2
    g_ref[...] = jnp.where(e == i1, 1.0 / den, 0.0) + jnp.where(e == i2, w2 / den, 0.0)


def _router(x, nw, rwt, bm=512):
    n, d = x.shape
    return pl.pallas_call(
        _router_kernel,
        grid=(n // bm,),
        in_specs=[pl.BlockSpec((bm, d), lambda i: (i, 0)),
                  pl.BlockSpec((1, d), lambda i: (0, 0)),
                  pl.BlockSpec((N_EXPERTS, d), lambda i: (0, 0))],
        out_specs=(pl.BlockSpec((N_EXPERTS, bm), lambda i: (0, i)),
                   pl.BlockSpec((bm, d), lambda i: (i, 0))),
        out_shape=(jax.ShapeDtypeStruct((N_EXPERTS, n), F32),
                   jax.ShapeDtypeStruct((n, d), BF16)),
        compiler_params=_params(("parallel",)),
        name="moe_router",
    )(x, nw.reshape(1, d), rwt)


def _moe_rank_kernel(g_ref, rank_ref, carry_ref):
    @pl.when(pl.program_id(0) == 0)
    def _():
        carry_ref[...] = jnp.zeros(carry_ref.shape, F32)

    tc = g_ref.shape[1]
    m = jnp.where(g_ref[...] > 0.0, 1.0, 0.0)
    r = lax.broadcasted_iota(jnp.int32, (tc, tc), 0)
    c = lax.broadcasted_iota(jnp.int32, (tc, tc), 1)
    before = jnp.where(r < c, 1.0, 0.0).astype(BF16)
    rank_ref[...] = jnp.dot(m.astype(BF16), before, preferred_element_type=F32) + carry_ref[:, :1]
    carry_ref[...] = carry_ref[...] + jnp.sum(m, axis=1, keepdims=True)


def _moe_rank(gates, tc):
    e, n = gates.shape
    return pl.pallas_call(
        _moe_rank_kernel,
        grid=(n // tc,),
        in_specs=[pl.BlockSpec((e, tc), lambda i: (0, i))],
        out_specs=pl.BlockSpec((e, tc), lambda i: (0, i)),
        out_shape=jax.ShapeDtypeStruct((e, n), F32),
        scratch_shapes=[pltpu.VMEM((e, LANE), F32)],
        compiler_params=_params(("arbitrary",)),
        name="moe_rank",
    )(gates)


def _moe_plan(rank_all, gates, tc, tm):
    e_n, n = rank_all.shape
    nc = n // tc
    nt_max = 2 * n // tm + e_n
    ni_max = e_n * nc + nt_max
    i32 = jnp.int32
    cstart = rank_all[:, ::tc].astype(i32)
    counts = (rank_all[:, -1] + jnp.where(gates[:, -1] > 0.0, 1.0, 0.0)).astype(i32)
    cend = jnp.concatenate([cstart[:, 1:], counts[:, None]], axis=1)
    tiles = (counts + tm - 1) // tm
    tend = jnp.cumsum(tiles)
    tstart = tend - tiles
    total = tend[-1]
    t_idx = jnp.arange(nt_max, dtype=i32)
    tvalid = t_idx < total
    texp = jnp.sum((t_idx[:, None] >= tend[None, :]).astype(i32), axis=1)
    last_exp = jnp.sum((total - 1 >= tend).astype(i32))
    texp = jnp.where(tvalid, texp, last_exp)
    tsrc = jnp.where(tvalid, t_idx, 0)

    nonempty = cend > cstart
    t0 = cstart // tm
    t1 = (cend - 1) // tm
    local = jnp.stack([t0, t0 + 1], axis=-1)
    ok = jnp.stack([nonempty, nonempty & (t1 > t0)], axis=-1).reshape(-1)
    shape = local.shape
    tile = (tstart[:, None, None] + local).reshape(-1)
    base = (local * tm).reshape(-1)
    exp = jnp.broadcast_to(jnp.arange(e_n, dtype=i32)[:, None, None], shape).reshape(-1)
    chunk = jnp.broadcast_to(jnp.arange(nc, dtype=i32)[None, :, None], shape).reshape(-1)
    n_items = jnp.sum(ok.astype(i32))
    k_idx = jnp.arange(ni_max, dtype=i32)
    valid = k_idx < n_items

    def ordered(key, group):
        order = jnp.argsort(jnp.where(ok, key, jnp.int32(2 ** 30)))[:ni_max]
        src = jnp.where(valid, order, order[n_items - 1])
        grp = group[src]
        first = jnp.concatenate([jnp.ones((1,), bool), grp[1:] != grp[:-1]]) & valid
        return (tile[src], chunk[src], exp[src], base[src], first.astype(i32), valid.astype(i32))

    by_tile = ordered(tile * nc + chunk, tile)
    by_chunk = ordered(chunk * nt_max + tile, chunk)
    return (texp, tvalid.astype(i32), tsrc), by_tile, by_chunk, nt_max, ni_max


def _moe_gather_kernel(tile_s, chunk_s, exp_s, base_s, first_s, valid_s, u_ref, rank_ref, g_ref, x_ref, gs_ref):
    k = pl.program_id(0)
    tm, tc = x_ref.shape[0], u_ref.shape[0]

    @pl.when(valid_s[k] == 1)
    def _():
        e = exp_s[k]
        gate = g_ref[pl.ds(e, 1), :]
        rank = jnp.where(gate > 0.0, rank_ref[pl.ds(e, 1), :], -1.0)
        rows = (base_s[k] + lax.broadcasted_iota(jnp.int32, (tm, tc), 0)).astype(F32)
        p = jnp.where(rows == rank, 1.0, 0.0)
        xs = jnp.dot(p.astype(BF16), u_ref[...], preferred_element_type=F32).astype(BF16)
        gsel = jnp.sum(p * gate, axis=1, keepdims=True)

        @pl.when(first_s[k] == 1)
        def _():
            x_ref[...] = xs
            gs_ref[...] = gsel

        @pl.when(first_s[k] == 0)
        def _():
            x_ref[...] = x_ref[...] + xs
            gs_ref[...] = gs_ref[...] + gsel


def _moe_gather(u, rank_all, gates, items, nt_max, ni_max, tc, tm):
    n, d = u.shape
    e = gates.shape[0]
    spec = pltpu.PrefetchScalarGridSpec(
        num_scalar_prefetch=6,
        grid=(ni_max,),
        in_specs=[pl.BlockSpec((tc, d), lambda k, t, c, *_: (c[k], 0)),
                  pl.BlockSpec((e, tc), lambda k, t, c, *_: (0, c[k])),
                  pl.BlockSpec((e, tc), lambda k, t, c, *_: (0, c[k]))],
        out_specs=(pl.BlockSpec((tm, d), lambda k, t, c, *_: (t[k], 0)),
                   pl.BlockSpec((tm, 1), lambda k, t, c, *_: (t[k], 0))),
    )
    return pl.pallas_call(
        _moe_gather_kernel,
        grid_spec=spec,
        out_shape=(jax.ShapeDtypeStruct((nt_max * tm, d), BF16),
                   jax.ShapeDtypeStruct((nt_max * tm, 1), F32)),
        compiler_params=_params(("arbitrary",)),
        name="moe_gather",
    )(*items, u, rank_all, gates)


def _moe_ffn_kernel(texp_s, tvalid_s, tsrc_s, x_ref, gs_ref, wg_ref, wu_ref, wd_ref, o_ref, acc_ref):
    t = pl.program_id(0)
    j = pl.program_id(1)
    valid = tvalid_s[t] == 1

    @pl.when(valid)
    def _():
        x = x_ref[...]
        g = jnp.dot(x, wg_ref[0], preferred_element_type=F32)
        up = jnp.dot(x, wu_ref[0], preferred_element_type=F32)
        hm = (_silu_mul(g, up) * gs_ref[...]).astype(BF16)
        y = jnp.dot(hm, wd_ref[0], preferred_element_type=F32)

        @pl.when(j == 0)
        def _():
            acc_ref[...] = y

        @pl.when(j > 0)
        def _():
            acc_ref[...] += y

    @pl.when(j == pl.num_programs(1) - 1)
    def _():
        @pl.when(valid)
        def _():
            o_ref[...] = acc_ref[...].astype(o_ref.dtype)

        @pl.when(jnp.logical_not(valid))
        def _():
            o_ref[...] = jnp.zeros(o_ref.shape, o_ref.dtype)


def _moe_ffn(xs, gs, tiles, wgu, wd, nt_max, tm, fc=1408):
    d = xs.shape[1]
    nj = D_FF // fc
    spec = pltpu.PrefetchScalarGridSpec(
        num_scalar_prefetch=3,
        grid=(nt_max, nj),
        in_specs=[pl.BlockSpec((tm, d), lambda t, j, te, tv, ts: (ts[t], 0)),
                  pl.BlockSpec((tm, 1), lambda t, j, te, tv, ts: (ts[t], 0)),
                  pl.BlockSpec((1, d, fc), lambda t, j, te, tv, ts: (te[t], 0, j)),
                  pl.BlockSpec((1, d, fc), lambda t, j, te, tv, ts: (te[t], 0, j + nj)),
                  pl.BlockSpec((1, fc, d), lambda t, j, te, tv, ts: (te[t], j, 0))],
        out_specs=pl.BlockSpec((tm, d), lambda t, j, te, tv, ts: (t, 0)),
        scratch_shapes=[pltpu.VMEM((tm, d), F32)],
    )
    return pl.pallas_call(
        _moe_ffn_kernel,
        grid_spec=spec,
        out_shape=jax.ShapeDtypeStruct((nt_max * tm, d), BF16),
        compiler_params=_params(("arbitrary", "arbitrary")),
        name="moe_ffn",
    )(*tiles, xs, gs, wgu, wgu, wd)


def _moe_combine_kernel(tile_s, chunk_s, exp_s, base_s, first_s, valid_s, h_ref, rcol_ref, ys_ref, o_ref):
    k = pl.program_id(0)
    tc, tm = h_ref.shape[0], ys_ref.shape[0]

    @pl.when(first_s[k] == 1)
    def _():
        o_ref[...] = h_ref[...]

    @pl.when(valid_s[k] == 1)
    def _():
        lane = lax.broadcasted_iota(jnp.int32, rcol_ref.shape, 1)
        rank = jnp.sum(jnp.where(lane == exp_s[k], rcol_ref[...], 0.0), axis=1, keepdims=True)
        cols = (base_s[k] + lax.broadcasted_iota(jnp.int32, (tc, tm), 1)).astype(F32)
        p = jnp.where(cols == rank, 1.0, 0.0).astype(BF16)
        o_ref[...] += jnp.dot(p, ys_ref[...], preferred_element_type=F32)


def _moe_combine(h, rank_col, ys, items, ni_max, tc, tm):
    n, d = h.shape
    e = rank_col.shape[1]
    spec = pltpu.PrefetchScalarGridSpec(
        num_scalar_prefetch=6,
        grid=(ni_max,),
        in_specs=[pl.BlockSpec((tc, d), lambda k, t, c, *_: (c[k], 0)),
                  pl.BlockSpec((tc, e), lambda k, t, c, *_: (c[k], 0)),
                  pl.BlockSpec((tm, d), lambda k, t, c, *_: (t[k], 0))],
        out_specs=pl.BlockSpec((tc, d), lambda k, t, c, *_: (c[k], 0)),
    )
    return pl.pallas_call(
        _moe_combine_kernel,
        grid_spec=spec,
        out_shape=jax.ShapeDtypeStruct((n, d), F32),
        compiler_params=_params(("arbitrary",)),
        name="moe_combine",
    )(*items, h, rank_col, ys)


def _moe_grouped(h, nw, rwt, wgu, wd, tc=512, tm=512):
    gates, u = _router(h, nw, rwt)
    rank_all = _moe_rank(gates, tc)
    tiles, by_tile, by_chunk, nt_max, ni_max = _moe_plan(rank_all, gates, tc, tm)
    xs, gs = _moe_gather(u, rank_all, gates, by_tile, nt_max, ni_max, tc, tm)
    ys = _moe_ffn(xs, gs, tiles, wgu, wd, nt_max, tm)
    rank_col = jnp.where(gates > 0.0, rank_all, -1.0).T
    return _moe_combine(h, rank_col, ys, by_chunk, ni_max, tc, tm)


def kernel(x, a_norm_w, a_w_in, a_q_norm_w, a_k_norm_w, a_w_out, kv_norm_w, kv_w, b_norm_w, b_w_q, b_w_out,
           ffn_norm_w, dense_w_gate_up, dense_w_down, moe_router_w, moe_w_gate_up, moe_w_down):
    b, s, d = x.shape
    n = b * s
    topk = min(MAX_TOPK, s // 4)
    tq_a, ch_a, tq_b = 256, 256, 256
    h = x.reshape(n, d)

    w_in = jnp.pad(a_w_in[0], ((0, 0), (0, A_IN_PAD - A_IN))).astype(BF16)
    proj = _norm_matmul(h, a_norm_w[0], w_in, F32)
    q, k, vt, qit, ki, wt = _dsa_prep(proj, a_q_norm_w[0], a_k_norm_w[0], b, s, tq_a, ch_a)
    o = _dsa_attn(qit, wt, ki, q, k, vt, b, s, tq_a, ch_a, topk)
    h = _matmul_residual(o, a_w_out[0].astype(BF16), h)
    h = _ffn_dense(h, ffn_norm_w[0], dense_w_gate_up[0].astype(BF16), dense_w_down[0].astype(BF16))

    kv = _norm_matmul(h, kv_norm_w, kv_w.astype(BF16), BF16)
    vt_sb = _transpose_chunks(kv, d, 1, b, s, tq_b)

    q_sb = _norm_matmul(h, b_norm_w[0], (b_w_q[0] * (HEAD_DIM ** -0.5 * LOG2E)).astype(BF16), BF16)
    o = _sb_attn(q_sb, kv, vt_sb, b, s, tq_b)
    h = _matmul_residual(o, b_w_out[0].astype(BF16), h)
    h = _moe_grouped(h, ffn_norm_w[1], moe_router_w[0].T,
                     moe_w_gate_up[0].astype(BF16), moe_w_down[0].astype(BF16))
    return h.reshape(b, s, d)
```

```python
import functools

import jax
import jax.numpy as jnp
from jax import lax
from jax.experimental import pallas as pl
from jax.experimental.pallas import tpu as pltpu

D_MODEL = 1024
HEAD_DIM = 128
N_HEADS_A = 8
N_KV_HEADS_A = 2
GROUP_A = 4
N_IDX_HEADS = 8
IDX_DIM = 64
MAX_TOPK = 256
N_HEADS_B = 8
ROPE_THETA = 500000.0
ROT_FRAC = 4
NORM_EPS = 1e-6
D_FF = 2816
N_EXPERTS = 8

A_Q = N_HEADS_A * HEAD_DIM
A_K = N_KV_HEADS_A * HEAD_DIM
A_V = N_KV_HEADS_A * HEAD_DIM
A_QI = N_IDX_HEADS * IDX_DIM
A_IN = A_Q + A_K + A_V + A_QI + IDX_DIM + N_IDX_HEADS
A_IN_PAD = A_Q + A_K + A_V + A_QI + 128

LANE = 128
INT_MIN = -(2 ** 31)
HALF_MIN = -(2 ** 15)
HALF_ROWS = 16
VT_ROWS = HEAD_DIM + HALF_ROWS
LOG2E = 1.4426950408889634
SB_DEAD_LOG2 = 151.0
VMEM_LIMIT = 56 * 1024 * 1024

F32 = jnp.float32
BF16 = jnp.bfloat16
NT_DIMS = (((1,), (1,)), ((), ()))


def _params(sem):
    return pltpu.CompilerParams(dimension_semantics=sem, vmem_limit_bytes=VMEM_LIMIT)


def _rms(x, w):
    ms = jnp.mean(x * x, axis=-1, keepdims=True)
    return x * lax.rsqrt(ms + NORM_EPS) * w


def _norm_matmul_kernel(x_ref, nw_ref, w_ref, o_ref):
    u = _rms(x_ref[...], nw_ref[...]).astype(BF16)
    o_ref[...] = jnp.dot(u, w_ref[...], preferred_element_type=F32).astype(o_ref.dtype)


def _norm_matmul(x, nw, w, out_dtype, bm=512):
    n, d = x.shape
    m = w.shape[1]
    return pl.pallas_call(
        _norm_matmul_kernel,
        grid=(n // bm,),
        in_specs=[pl.BlockSpec((bm, d), lambda i: (i, 0)),
                  pl.BlockSpec((1, d), lambda i: (0, 0)),
                  pl.BlockSpec((d, m), lambda i: (0, 0))],
        out_specs=pl.BlockSpec((bm, m), lambda i: (i, 0)),
        out_shape=jax.ShapeDtypeStruct((n, m), out_dtype),
        compiler_params=_params(("parallel",)),
        name="norm_matmul",
    )(x, nw.reshape(1, d), w)


def _matmul_res_kernel(a_ref, w_ref, r_ref, o_ref):
    o_ref[...] = r_ref[...] + jnp.dot(a_ref[...], w_ref[...], preferred_element_type=F32)


def _matmul_residual(a, w, res, bm=512):
    n, k = a.shape
    m = w.shape[1]
    return pl.pallas_call(
        _matmul_res_kernel,
        grid=(n // bm,),
        in_specs=[pl.BlockSpec((bm, k), lambda i: (i, 0)),
                  pl.BlockSpec((k, m), lambda i: (0, 0)),
                  pl.BlockSpec((bm, m), lambda i: (i, 0))],
        out_specs=pl.BlockSpec((bm, m), lambda i: (i, 0)),
        out_shape=jax.ShapeDtypeStruct((n, m), F32),
        compiler_params=_params(("parallel",)),
        name="matmul_residual",
    )(a, w, res)


def _rope(x, c, s_lo, s_hi, half):
    n = x.shape[-1]
    return x * c + pltpu.roll(x, n - half, 1) * s_lo + pltpu.roll(x, half, 1) * s_hi


def _dsa_prep_kernel(p_ref, qw_ref, kw_ref, c_ref, slo_ref, shi_ref, ci_ref, sloi_ref, shii_ref,
                     q_ref, k_ref, vt_ref, qit_ref, ki_ref, wt_ref, *, tq):
    bm = p_ref.shape[0]
    c, slo, shi = c_ref[...], slo_ref[...], shi_ref[...]
    ci, sloi, shii = ci_ref[...], sloi_ref[...], shii_ref[...]
    scale = HEAD_DIM ** -0.5 * LOG2E
    for h in range(N_HEADS_A):
        xh = _rms(p_ref[:, h * HEAD_DIM:(h + 1) * HEAD_DIM], qw_ref[...])
        q_ref[:, h * HEAD_DIM:(h + 1) * HEAD_DIM] = (_rope(xh, c, slo, shi, 16) * scale).astype(BF16)
    for h in range(N_KV_HEADS_A):
        xh = _rms(p_ref[:, A_Q + h * HEAD_DIM:A_Q + (h + 1) * HEAD_DIM], kw_ref[...])
        k_ref[:, h * HEAD_DIM:(h + 1) * HEAD_DIM] = _rope(xh, c, slo, shi, 16).astype(BF16)
    ones = jnp.ones((VT_ROWS - HEAD_DIM, bm), BF16)
    for h in range(N_KV_HEADS_A):
        v = p_ref[:, A_Q + A_K + h * HEAD_DIM:A_Q + A_K + (h + 1) * HEAD_DIM]
        vt_ref[0, 0, h * VT_ROWS:h * VT_ROWS + HEAD_DIM, :] = v.T.astype(BF16)
        vt_ref[0, 0, h * VT_ROWS + HEAD_DIM:(h + 1) * VT_ROWS, :] = ones
    off = A_Q + A_K + A_V
    for j in range(A_QI // LANE):
        xj = _rope(p_ref[:, off + j * LANE:off + (j + 1) * LANE], ci, sloi, shii, 8)
        xt = xj.T.astype(BF16)
        for b in range(bm // tq):
            qit_ref[0, b, j * LANE:(j + 1) * LANE, :] = xt[:, b * tq:(b + 1) * tq]
    off += A_QI
    lane = lax.broadcasted_iota(jnp.int32, (1, LANE), 1)
    first = lane < IDX_DIM
    kw = p_ref[:, off:off + LANE]
    kw = _rope(kw, jnp.where(first, ci, 1.0), jnp.where(first, sloi, 0.0), jnp.where(first, shii, 0.0), 8)
    ki_ref[...] = kw[:, :IDX_DIM].astype(BF16)
    wt = kw.T[IDX_DIM:IDX_DIM + N_IDX_HEADS, :] * (N_IDX_HEADS ** -0.5) * (IDX_DIM ** -0.5)
    for b in range(bm // tq):
        wt_ref[0, b] = wt[:, b * tq:(b + 1) * tq]


def _rope_tables(s, head, reps):
    r = head // ROT_FRAC
    half = r // 2
    inv = ROPE_THETA ** (-jnp.arange(half, dtype=F32) * 2.0 / r)
    ang = jnp.arange(s).astype(F32)[:, None] * inv[None, :]
    cos, sin = jnp.cos(ang), jnp.sin(ang)
    one = jnp.ones((s, head - r), F32)
    zero = jnp.zeros((s, head - r), F32)
    zh = jnp.zeros((s, half), F32)
    c = jnp.concatenate([cos, cos, one], axis=1)
    s_lo = jnp.concatenate([-sin, zh, zero], axis=1)
    s_hi = jnp.concatenate([zh, sin, zero], axis=1)
    return tuple(jnp.tile(t, (1, reps)) for t in (c, s_lo, s_hi))


def _dsa_prep(proj, qw, kw, b, s, tq, ch):
    n = proj.shape[0]
    bm = ch
    nb = s // bm
    c, slo, shi = _rope_tables(s, HEAD_DIM, 1)
    ci, sloi, shii = _rope_tables(s, IDX_DIM, 2)
    tab = pl.BlockSpec((bm, LANE), lambda i: (i % nb, 0))
    row = lambda w: pl.BlockSpec((bm, w), lambda i: (i, 0))
    out_shapes = (
        jax.ShapeDtypeStruct((n, A_Q), BF16),
        jax.ShapeDtypeStruct((n, A_K), BF16),
        jax.ShapeDtypeStruct((b, s // ch, N_KV_HEADS_A * VT_ROWS, ch), BF16),
        jax.ShapeDtypeStruct((b, s // tq, A_QI, tq), BF16),
        jax.ShapeDtypeStruct((n, IDX_DIM), BF16),
        jax.ShapeDtypeStruct((b, s // tq, N_IDX_HEADS, tq), F32),
    )
    out_specs = (
        row(A_Q), row(A_K),
        pl.BlockSpec((1, 1, N_KV_HEADS_A * VT_ROWS, ch), lambda i: (i // nb, i % nb, 0, 0)),
        pl.BlockSpec((1, bm // tq, A_QI, tq), lambda i: (i // nb, i % nb, 0, 0)),
        row(IDX_DIM),
        pl.BlockSpec((1, bm // tq, N_IDX_HEADS, tq), lambda i: (i // nb, i % nb, 0, 0)),
    )
    return pl.pallas_call(
        functools.partial(_dsa_prep_kernel, tq=tq),
        grid=(n // bm,),
        in_specs=[row(A_IN_PAD),
                  pl.BlockSpec((1, HEAD_DIM), lambda i: (0, 0)),
                  pl.BlockSpec((1, HEAD_DIM), lambda i: (0, 0)),
                  tab, tab, tab, tab, tab, tab],
        out_specs=out_specs,
        out_shape=out_shapes,
        compiler_params=_params(("parallel",)),
        name="dsa_prep",
    )(proj, qw.reshape(1, -1), kw.reshape(1, -1), c, slo, shi, ci, sloi, shii)


def _dsa_attn_kernel(qit_ref, wt_ref, ki_ref, q_ref, k_ref, vt_ref, o_ref,
                     keys_ref, hi_ref, lo_ref, thr_ref, cnt_ref, acc_ref, *, tq, ch, topk):
    i = pl.program_id(1)
    nch = ((i + 1) * tq + ch - 1) // ch
    t_row = i * tq + lax.broadcasted_iota(jnp.int32, (1, tq), 1)

    qit = jnp.concatenate([qit_ref[0, 0, h * IDX_DIM:(h + 1) * IDX_DIM, :] for h in range(N_IDX_HEADS)],
                          axis=1)
    wt = wt_ref[0, 0]

    npair = (nch + 1) // 2

    def score_body(c2, carry):
        starts = [pl.multiple_of(c2 * (2 * ch) + half * ch, ch) for half in range(2)]
        rels = [jnp.dot(ki_ref[pl.ds(s0, ch), :], qit, preferred_element_type=F32) for s0 in starts]
        for s0, rel in zip(starts, rels):
            sc = jnp.zeros((ch, tq), F32)
            for h in range(N_IDX_HEADS):
                sc = sc + jnp.maximum(rel[:, h * tq:(h + 1) * tq], 0.0) * wt[h:h + 1, :]
            sc = jnp.where(sc == 0.0, 0.0, sc)
            bits = lax.bitcast_convert_type(sc, jnp.int32)
            key = bits ^ ((bits >> 31) & 0x7FFFFFFF)
            s_idx = s0 + lax.broadcasted_iota(jnp.int32, (ch, tq), 0)
            key = jnp.where(s_idx <= t_row, key, INT_MIN)
            keys_ref[pl.ds(s0, ch), :] = key
            hi_ref[pl.ds(s0, ch), :] = (key >> 16).astype(jnp.int16)
        return carry

    lax.fori_loop(0, npair, score_body, 0)

    def count16(ref, thr):
        t16 = jnp.broadcast_to(thr.astype(jnp.int16), (HALF_ROWS, tq))

        def body(c2, acc):
            s0 = pl.multiple_of(c2 * (2 * ch), 2 * ch)
            x = ref[pl.ds(s0, 2 * ch), :].reshape(2 * ch // HALF_ROWS, HALF_ROWS, tq)
            hit = jnp.where(x >= t16[None], jnp.int16(1), jnp.int16(0))
            parts = [hit[r] for r in range(2 * ch // HALF_ROWS)]
            while len(parts) > 1:
                parts = [a + b for a, b in zip(parts[0::2], parts[1::2])]
            return acc + parts[0]

        acc = lax.fori_loop(0, npair, body, jnp.zeros((HALF_ROWS, tq), jnp.int16))
        return jnp.sum(acc.astype(jnp.int32), axis=0, keepdims=True)

    def bisect_pass(ref, want, bit, st):
        thr, cnt, over = st
        cand = thr + (jnp.int32(1) << bit)
        c = count16(ref, cand)
        take = c >= want
        return jnp.where(take, cand, thr), jnp.where(take, c, cnt), jnp.where(take, over, c)

    def bisect_start(ref, want, cnt_floor):
        c0 = count16(ref, jnp.zeros((1, tq), jnp.int32))
        pos = c0 >= want
        return jnp.where(pos, 0, HALF_MIN), jnp.where(pos, c0, cnt_floor), jnp.where(pos, 0, c0)

    short = t_row < topk
    want_hi = jnp.full((1, tq), topk, jnp.int32)
    st = bisect_start(hi_ref, want_hi, jnp.full((1, tq), npair * 2 * ch, jnp.int32))
    thr_hi, cnt_hi, above = lax.fori_loop(0, 15, lambda it, s: bisect_pass(hi_ref, want_hi, 14 - it, s), st)
    want_lo = topk - above

    def lo_body(c2, carry):
        s0 = pl.multiple_of(c2 * (2 * ch), 2 * ch)
        key = keys_ref[pl.ds(s0, 2 * ch), :]
        lo = (key & 0xFFFF) + HALF_MIN
        lo_ref[pl.ds(s0, 2 * ch), :] = jnp.where((key >> 16) == thr_hi, lo, HALF_MIN).astype(jnp.int16)
        return carry

    lax.fori_loop(0, npair, lo_body, 0)

    def n_open(cnt):
        return jnp.sum(jnp.where(jnp.logical_and(jnp.logical_not(short), cnt != want_lo), 1, 0))

    def lo_cond(s):
        return jnp.logical_and(s[0] < 15, s[2] > 0)

    def lo_step(s):
        it, st, _ = s
        for k in range(3):
            st = bisect_pass(lo_ref, want_lo, 14 - it - k, st)
        return it + 3, st, n_open(st[1])

    st = bisect_start(lo_ref, want_lo, cnt_hi - above)
    _, (thr_lo, cnt_lo, _), left = lax.while_loop(lo_cond, lo_step, (jnp.int32(0), st, n_open(st[1])))
    thr_ref[...] = thr_hi * 65536 + (thr_lo - HALF_MIN)
    cnt_ref[...] = above + cnt_lo

    def open_rows():
        return jnp.logical_and(jnp.logical_not(short), cnt_ref[...] != topk)

    @pl.when(left > 0)
    def _():
        thr = jnp.where(open_rows(), thr_ref[...], INT_MIN)

        def gt_body(c, acc):
            s0 = pl.multiple_of(c * ch, ch)
            hit = jnp.where(keys_ref[pl.ds(s0, ch), :] > thr, 1, 0)
            return acc + jnp.sum(hit.reshape(ch // 8, 8, tq), axis=0)
        gt = jnp.sum(lax.fori_loop(0, nch, gt_body, jnp.zeros((8, tq), jnp.int32)), axis=0, keepdims=True)
        need = (topk - gt).astype(F32)
        r = lax.broadcasted_iota(jnp.int32, (ch, ch), 0)
        cc = lax.broadcasted_iota(jnp.int32, (ch, ch), 1)
        tri = jnp.where(cc <= r, 1.0, 0.0).astype(BF16)

        def tie_body(c, run):
            s0 = pl.multiple_of(c * ch, ch)
            kc = keys_ref[pl.ds(s0, ch), :]
            eq = kc == thr
            eqf = jnp.where(eq, 1.0, 0.0)
            pre = jnp.dot(tri, eqf.astype(BF16), preferred_element_type=F32) + run
            drop = jnp.logical_and(eq, pre > need)
            keys_ref[pl.ds(s0, ch), :] = jnp.where(drop, INT_MIN, kc)
            return run + jnp.sum(eqf, axis=0, keepdims=True)

        lax.fori_loop(0, nch, tie_body, jnp.zeros((1, tq), F32))

    thr = jnp.where(short, INT_MIN + 1, thr_ref[...])

    heads = range(N_HEADS_A)
    acc_ref[...] = jnp.zeros(acc_ref.shape, F32)

    def att_body(c2, carry):
        s0 = pl.multiple_of(c2 * (2 * ch), 2 * ch)
        bias = jnp.where(keys_ref[pl.ds(s0, 2 * ch), :] >= thr, 0.0, -jnp.inf)
        lgs = [lax.dot_general(k_ref[pl.ds(s0, 2 * ch), (h // GROUP_A) * HEAD_DIM:(h // GROUP_A + 1) * HEAD_DIM],
                               q_ref[:, h * HEAD_DIM:(h + 1) * HEAD_DIM], NT_DIMS,
                               preferred_element_type=F32) for h in heads]
        lgs = [(lg + bias).astype(BF16) for lg in lgs]
        m_new = [jnp.maximum(carry[h], jnp.max(lgs[h], axis=0, keepdims=True).astype(F32)) for h in heads]
        m_safe = [jnp.where(m == -jnp.inf, 0.0, m) for m in m_new]
        ps = [jnp.exp2(lgs[h] - m_safe[h].astype(BF16)) for h in heads]
        pvs = [sum(jnp.dot(vt_ref[0, 2 * c2 + half, (h // GROUP_A) * VT_ROWS:(h // GROUP_A + 1) * VT_ROWS, :],
                           ps[h][half * ch:(half + 1) * ch], preferred_element_type=F32) for half in range(2))
               for h in heads]
        for h in heads:
            acc_ref[h] = acc_ref[h] * jnp.exp2(carry[h] - m_safe[h]) + pvs[h]
        return tuple(m_new)

    m0 = jnp.full((1, tq), -jnp.inf, F32)
    lax.fori_loop(0, npair, att_body, (m0,) * N_HEADS_A)
    for h in heads:
        og = acc_ref[h, :HEAD_DIM, :] / acc_ref[h, HEAD_DIM:HEAD_DIM + 1, :]
        o_ref[:, h * HEAD_DIM:(h + 1) * HEAD_DIM] = og.T.astype(o_ref.dtype)


def _dsa_attn(qit, wt, ki, q, k, vt, b, s, tq, ch, topk):
    nq = s // tq
    return pl.pallas_call(
        functools.partial(_dsa_attn_kernel, tq=tq, ch=ch, topk=topk),
        grid=(b, nq),
        in_specs=[pl.BlockSpec((1, 1, A_QI, tq), lambda bb, i: (bb, i, 0, 0)),
                  pl.BlockSpec((1, 1, N_IDX_HEADS, tq), lambda bb, i: (bb, i, 0, 0)),
                  pl.BlockSpec((s, IDX_DIM), lambda bb, i: (bb, 0)),
                  pl.BlockSpec((tq, A_Q), lambda bb, i: (bb * nq + i, 0)),
                  pl.BlockSpec((s, A_K), lambda bb, i: (bb, 0)),
                  pl.BlockSpec((1, s // ch, N_KV_HEADS_A * VT_ROWS, ch), lambda bb, i: (bb, 0, 0, 0))],
        out_specs=pl.BlockSpec((tq, A_Q), lambda bb, i: (bb * nq + i, 0)),
        out_shape=jax.ShapeDtypeStruct((b * s, A_Q), BF16),
        scratch_shapes=[pltpu.VMEM((s, tq), jnp.int32),
                        pltpu.VMEM((s, tq), jnp.int16),
                        pltpu.VMEM((s, tq), jnp.int16),
                        pltpu.VMEM((1, tq), jnp.int32),
                        pltpu.VMEM((1, tq), jnp.int32),
                        pltpu.VMEM((N_HEADS_A, VT_ROWS, tq), F32)],
        compiler_params=_params(("parallel", "arbitrary")),
        name="dsa_attn",
    )(qit, wt, ki, q, k, vt)


def _sb_attn_kernel(q_ref, k_ref, vt_ref, o_ref, acc_ref, *, tq, hb):
    i = pl.program_id(2)
    ch = tq
    t_row = i * tq + lax.broadcasted_iota(jnp.int32, (1, tq), 1)
    r = lax.broadcasted_iota(jnp.int32, (ch, ch), 0)
    cc = lax.broadcasted_iota(jnp.int32, (ch, ch), 1)
    tri = jnp.where(cc >= r, 1.0, 0.0).astype(BF16)
    acc_ref[...] = jnp.zeros(acc_ref.shape, F32)

    def cond(st):
        j, _, low = st
        return jnp.logical_and(j <= i, low <= SB_DEAD_LOG2)

    def body(st):
        j, runs, _ = st
        c = i - j
        s0 = pl.multiple_of(c * ch, ch)
        strict = (s0 + lax.broadcasted_iota(jnp.int32, (ch, tq), 0)) < t_row
        heads = [slice(h * HEAD_DIM, (h + 1) * HEAD_DIM) for h in range(hb)]
        zs = [lax.dot_general(k_ref[pl.ds(s0, ch), hs], q_ref[:, hs], NT_DIMS, preferred_element_type=F32)
              for hs in heads]
        sps = [jnp.where(strict, jnp.maximum(z, 0.0) + jnp.log(1.0 + jnp.exp2(-jnp.abs(z))) * LOG2E, 0.0)
               for z in zs]
        his = [sp.astype(BF16) for sp in sps]
        los = [(sp - hi.astype(F32)).astype(BF16) for sp, hi in zip(sps, his)]
        cums = [jnp.dot(tri, hi, preferred_element_type=F32) + jnp.dot(tri, lo, preferred_element_type=F32) + run
                for hi, lo, run in zip(his, los, runs)]
        aas = [jnp.where(strict, jnp.exp2(z - cum), 0.0).astype(BF16) for z, cum in zip(zs, cums)]
        pvs = [jnp.dot(vt_ref[0, c, hs, :], a, preferred_element_type=F32) for hs, a in zip(heads, aas)]
        new_runs = [run + jnp.sum(sp, axis=0, keepdims=True) for run, sp in zip(runs, sps)]
        for h in range(hb):
            acc_ref[h] += pvs[h]
        low = jnp.min(functools.reduce(jnp.minimum, new_runs))
        return j + 1, tuple(new_runs), low

    run0 = tuple(jnp.zeros((1, tq), F32) for _ in range(hb))
    lax.while_loop(cond, body, (jnp.int32(0), run0, jnp.float32(0.0)))
    for h in range(hb):
        o_ref[:, h * HEAD_DIM:(h + 1) * HEAD_DIM] = acc_ref[h].T.astype(o_ref.dtype)


def _sb_attn(q, k, vt, b, s, tq, hb=4):
    nq = s // tq
    d = N_HEADS_B * HEAD_DIM
    w = hb * HEAD_DIM
    return pl.pallas_call(
        functools.partial(_sb_attn_kernel, tq=tq, hb=hb),
        grid=(b, N_HEADS_B // hb, nq),
        in_specs=[pl.BlockSpec((tq, w), lambda bb, h, i: (bb * nq + i, h)),
                  pl.BlockSpec((s, w), lambda bb, h, i: (bb, h)),
                  pl.BlockSpec((1, s // tq, w, tq), lambda bb, h, i: (bb, 0, h, 0))],
        out_specs=pl.BlockSpec((tq, w), lambda bb, h, i: (bb * nq + i, h)),
        out_shape=jax.ShapeDtypeStruct((b * s, d), BF16),
        scratch_shapes=[pltpu.VMEM((hb, HEAD_DIM, tq), F32)],
        compiler_params=_params(("parallel", "parallel", "arbitrary")),
        name="sb_attn",
    )(q, k, vt)


def _transpose_chunks_kernel(x_ref, o_ref):
    o_ref[0, 0] = x_ref[...].astype(F32).T.astype(o_ref.dtype)


def _transpose_chunks(x, d, col, b, s, ch):
    n = x.shape[0]
    nb = s // ch
    return pl.pallas_call(
        _transpose_chunks_kernel,
        grid=(n // ch,),
        in_specs=[pl.BlockSpec((ch, d), lambda i: (i, col))],
        out_specs=pl.BlockSpec((1, 1, d, ch), lambda i: (i // nb, i % nb, 0, 0)),
        out_shape=jax.ShapeDtypeStruct((b, nb, d, ch), x.dtype),
        compiler_params=_params(("parallel",)),
        name="transpose_chunks",
    )(x)


def _silu_mul(g, up):
    return g * (1.0 / (1.0 + jnp.exp(-g))) * up


def _ffn_kernel(x_ref, nw_ref, wg_ref, wu_ref, wd_ref, o_ref, u_ref):
    j = pl.program_id(1)

    @pl.when(j == 0)
    def _():
        x = x_ref[...]
        u_ref[...] = _rms(x, nw_ref[...]).astype(BF16)
        o_ref[...] = x

    u = u_ref[...]
    g = jnp.dot(u, wg_ref[...], preferred_element_type=F32)
    up = jnp.dot(u, wu_ref[...], preferred_element_type=F32)
    o_ref[...] += jnp.dot(_silu_mul(g, up).astype(BF16), wd_ref[...], preferred_element_type=F32)


def _ffn_dense(x, nw, wgu, wd, bm=512, fc=1408):
    n, d = x.shape
    nj = D_FF // fc
    return pl.pallas_call(
        _ffn_kernel,
        grid=(n // bm, nj),
        in_specs=[pl.BlockSpec((bm, d), lambda i, j: (i, 0)),
                  pl.BlockSpec((1, d), lambda i, j: (0, 0)),
                  pl.BlockSpec((d, fc), lambda i, j: (0, j)),
                  pl.BlockSpec((d, fc), lambda i, j: (0, j + nj)),
                  pl.BlockSpec((fc, d), lambda i, j: (j, 0))],
        out_specs=pl.BlockSpec((bm, d), lambda i, j: (i, 0)),
        out_shape=jax.ShapeDtypeStruct((n, d), F32),
        scratch_shapes=[pltpu.VMEM((bm, d), BF16)],
        compiler_params=_params(("parallel", "arbitrary")),
        name="ffn_dense",
    )(x, nw.reshape(1, d), wgu, wgu, wd)


def _router_kernel(x_ref, nw_ref, rwt_ref, g_ref, u_ref):
    u = _rms(x_ref[...], nw_ref[...])
    u_ref[...] = u.astype(BF16)
    lt = lax.dot_general(rwt_ref[...], u, NT_DIMS, preferred_element_type=F32,
                         precision=lax.Precision.HIGHEST)
    e = lax.broadcasted_iota(jnp.int32, lt.shape, 0)
    m1 = jnp.max(lt, axis=0, keepdims=True)
    i1 = jnp.min(jnp.where(lt == m1, e, N_EXPERTS), axis=0, keepdims=True)
    rest = jnp.where(e == i1, -jnp.inf, lt)
    m2 = jnp.max(rest, axis=0, keepdims=True)
    i2 = jnp.min(jnp.where(rest == m2, e, N_EXPERTS), axis=0, keepdims=True)
    w2 = jnp.exp(m2 - m1)
    den = 1.0 + w2
    g_ref[...] = jnp.where(e == i1, 1.0 / den, 0.0) + jnp.where(e == i2, w2 / den, 0.0)


def _router(x, nw, rwt, bm=512):
    n, d = x.shape
    return pl.pallas_call(
        _router_kernel,
        grid=(n // bm,),
        in_specs=[pl.BlockSpec((bm, d), lambda i: (i, 0)),
                  pl.BlockSpec((1, d), lambda i: (0, 0)),
                  pl.BlockSpec((N_EXPERTS, d), lambda i: (0, 0))],
        out_specs=(pl.BlockSpec((N_EXPERTS, bm), lambda i: (0, i)),
                   pl.BlockSpec((bm, d), lambda i: (i, 0))),
        out_shape=(jax.ShapeDtypeStruct((N_EXPERTS, n), F32),
                   jax.ShapeDtypeStruct((n, d), BF16)),
        compiler_params=_params(("parallel",)),
        name="moe_router",
    )(x, nw.reshape(1, d), rwt)


def _moe_rank_kernel(g_ref, rank_ref, carry_ref):
    @pl.when(pl.program_id(0) == 0)
    def _():
        carry_ref[...] = jnp.zeros(carry_ref.shape, F32)

    tc = g_ref.shape[1]
    m = jnp.where(g_ref[...] > 0.0, 1.0, 0.0)
    r = lax.broadcasted_iota(jnp.int32, (tc, tc), 0)
    c = lax.broadcasted_iota(jnp.int32, (tc, tc), 1)
    before = jnp.where(r < c, 1.0, 0.0).astype(BF16)
    rank_ref[...] = jnp.dot(m.astype(BF16), before, preferred_element_type=F32) + carry_ref[:, :1]
    carry_ref[...] = carry_ref[...] + jnp.sum(m, axis=1, keepdims=True)


def _moe_rank(gates, tc):
    e, n = gates.shape
    return pl.pallas_call(
        _moe_rank_kernel,
        grid=(n // tc,),
        in_specs=[pl.BlockSpec((e, tc), lambda i: (0, i))],
        out_specs=pl.BlockSpec((e, tc), lambda i: (0, i)),
        out_shape=jax.ShapeDtypeStruct((e, n), F32),
        scratch_shapes=[pltpu.VMEM((e, LANE), F32)],
        compiler_params=_params(("arbitrary",)),
        name="moe_rank",
    )(gates)


def _moe_plan(rank_all, gates, tc, tm, tr):
    e_n, n = rank_all.shape
    nc = n // tc
    nt_max = 2 * n // tm + e_n
    nr_max = nt_max * (tm // tr)
    ni_max = e_n * nc + nr_max
    slots = tc // tr + 1
    i32 = jnp.int32
    cstart = rank_all[:, ::tc].astype(i32)
    counts = (rank_all[:, -1] + jnp.where(gates[:, -1] > 0.0, 1.0, 0.0)).astype(i32)
    cend = jnp.concatenate([cstart[:, 1:], counts[:, None]], axis=1)
    tiles = (counts + tm - 1) // tm
    tend = jnp.cumsum(tiles)
    tstart = tend - tiles
    total = tend[-1]
    t_idx = jnp.arange(nt_max, dtype=i32)
    tvalid = t_idx < total
    texp = jnp.sum((t_idx[:, None] >= tend[None, :]).astype(i32), axis=1)
    last_exp = jnp.sum((total - 1 >= tend).astype(i32))
    texp = jnp.where(tvalid, texp, last_exp)
    tsrc = jnp.where(tvalid, t_idx, 0)

    nonempty = cend > cstart
    t0 = cstart // tr
    t1 = (cend - 1) // tr
    local = jnp.stack([t0 + j for j in range(slots)], axis=-1)
    ok = jnp.stack([nonempty & (t0 + j <= t1) for j in range(slots)], axis=-1).reshape(-1)
    shape = local.shape
    tile = (tstart[:, None, None] * (tm // tr) + local).reshape(-1)
    base = (local * tr).reshape(-1)
    exp = jnp.broadcast_to(jnp.arange(e_n, dtype=i32)[:, None, None], shape).reshape(-1)
    chunk = jnp.broadcast_to(jnp.arange(nc, dtype=i32)[None, :, None], shape).reshape(-1)
    n_items = jnp.sum(ok.astype(i32))
    k_idx = jnp.arange(ni_max, dtype=i32)
    valid = k_idx < n_items

    def ordered(key, group):
        order = jnp.argsort(jnp.where(ok, key, jnp.int32(2 ** 30)))[:ni_max]
        src = jnp.where(valid, order, order[n_items - 1])
        grp = group[src]
        first = jnp.concatenate([jnp.ones((1,), bool), grp[1:] != grp[:-1]]) & valid
        return (tile[src], chunk[src], exp[src], base[src], first.astype(i32), valid.astype(i32))

    by_tile = ordered(tile * nc + chunk, tile)
    by_chunk = ordered(chunk * nr_max + tile, chunk)
    return (texp, tvalid.astype(i32), tsrc), by_tile, by_chunk, nt_max, ni_max


def _moe_gather_kernel(tile_s, chunk_s, exp_s, base_s, first_s, valid_s, u_ref, rank_ref, g_ref,
                       x_zero_ref, gs_zero_ref, x_ref, gs_ref):
    del x_zero_ref, gs_zero_ref
    k = pl.program_id(0)
    tm, tc = x_ref.shape[0], u_ref.shape[0]

    @pl.when(valid_s[k] == 1)
    def _():
        e = exp_s[k]
        gate = g_ref[pl.ds(e, 1), :]
        rank = jnp.where(gate > 0.0, rank_ref[pl.ds(e, 1), :], -1.0)
        rows = (base_s[k] + lax.broadcasted_iota(jnp.int32, (tm, tc), 0)).astype(F32)
        p = jnp.where(rows == rank, 1.0, 0.0)
        xs = jnp.dot(p.astype(BF16), u_ref[...], preferred_element_type=F32).astype(BF16)
        gsel = jnp.sum(p * gate, axis=1, keepdims=True)

        @pl.when(first_s[k] == 1)
        def _():
            x_ref[...] = xs
            gs_ref[...] = gsel

        @pl.when(first_s[k] == 0)
        def _():
            x_ref[...] = x_ref[...] + xs
            gs_ref[...] = gs_ref[...] + gsel


def _moe_gather(u, rank_all, gates, items, n_rows, ni_max, tc, tr):
    n, d = u.shape
    e = gates.shape[0]
    n_prefetch = len(items)
    spec = pltpu.PrefetchScalarGridSpec(
        num_scalar_prefetch=n_prefetch,
        grid=(ni_max,),
        in_specs=[pl.BlockSpec((tc, d), lambda k, t, c, *_: (c[k], 0)),
                  pl.BlockSpec((e, tc), lambda k, t, c, *_: (0, c[k])),
                  pl.BlockSpec((e, tc), lambda k, t, c, *_: (0, c[k])),
                  pl.BlockSpec(memory_space=pl.ANY),
                  pl.BlockSpec(memory_space=pl.ANY)],
        out_specs=(pl.BlockSpec((tr, d), lambda k, t, c, *_: (t[k], 0)),
                   pl.BlockSpec((tr, 1), lambda k, t, c, *_: (t[k], 0))),
    )
    return pl.pallas_call(
        _moe_gather_kernel,
        grid_spec=spec,
        out_shape=(jax.ShapeDtypeStruct((n_rows, d), BF16),
                   jax.ShapeDtypeStruct((n_rows, 1), F32)),
        input_output_aliases={n_prefetch + 3: 0, n_prefetch + 4: 1},
        compiler_params=_params(("arbitrary",)),
        name="moe_gather",
    )(*items, u, rank_all, gates, jnp.zeros((n_rows, d), BF16), jnp.zeros((n_rows, 1), F32))


def _moe_ffn_kernel(texp_s, tvalid_s, tsrc_s, x_ref, gs_ref, wg_ref, wu_ref, wd_ref, o_ref, acc_ref):
    t = pl.program_id(0)
    j = pl.program_id(1)
    valid = tvalid_s[t] == 1

    @pl.when(valid)
    def _():
        x = x_ref[...]
        g = jnp.dot(x, wg_ref[0], preferred_element_type=F32)
        up = jnp.dot(x, wu_ref[0], preferred_element_type=F32)
        hm = (_silu_mul(g, up) * gs_ref[...]).astype(BF16)
        y = jnp.dot(hm, wd_ref[0], preferred_element_type=F32)

        @pl.when(j == 0)
        def _():
            acc_ref[...] = y

        @pl.when(j > 0)
        def _():
            acc_ref[...] += y

    @pl.when(j == pl.num_programs(1) - 1)
    def _():
        @pl.when(valid)
        def _():
            o_ref[...] = acc_ref[...].astype(o_ref.dtype)

        @pl.when(jnp.logical_not(valid))
        def _():
            o_ref[...] = jnp.zeros(o_ref.shape, o_ref.dtype)


def _moe_ffn(xs, gs, tiles, wgu, wd, nt_max, tm, fc=1408):
    d = xs.shape[1]
    nj = D_FF // fc
    spec = pltpu.PrefetchScalarGridSpec(
        num_scalar_prefetch=3,
        grid=(nt_max, nj),
        in_specs=[pl.BlockSpec((tm, d), lambda t, j, te, tv, ts: (ts[t], 0)),
                  pl.BlockSpec((tm, 1), lambda t, j, te, tv, ts: (ts[t], 0)),
                  pl.BlockSpec((1, d, fc), lambda t, j, te, tv, ts: (te[t], 0, j)),
                  pl.BlockSpec((1, d, fc), lambda t, j, te, tv, ts: (te[t], 0, j + nj)),
                  pl.BlockSpec((1, fc, d), lambda t, j, te, tv, ts: (te[t], j, 0))],
        out_specs=pl.BlockSpec((tm, d), lambda t, j, te, tv, ts: (t, 0)),
        scratch_shapes=[pltpu.VMEM((tm, d), F32)],
    )
    return pl.pallas_call(
        _moe_ffn_kernel,
        grid_spec=spec,
        out_shape=jax.ShapeDtypeStruct((nt_max * tm, d), BF16),
        compiler_params=_params(("arbitrary", "arbitrary")),
        name="moe_ffn",
    )(*tiles, xs, gs, wgu, wgu, wd)


def _moe_combine_kernel(tile_s, chunk_s, exp_s, base_s, first_s, valid_s, h_ref, rcol_ref, ys_ref, o_ref):
    k = pl.program_id(0)
    tc, tm = h_ref.shape[0], ys_ref.shape[0]

    @pl.when(first_s[k] == 1)
    def _():
        o_ref[...] = h_ref[...]

    @pl.when(valid_s[k] == 1)
    def _():
        lane = lax.broadcasted_iota(jnp.int32, rcol_ref.shape, 1)
        rank = jnp.sum(jnp.where(lane == exp_s[k], rcol_ref[...], 0.0), axis=1, keepdims=True)
        cols = (base_s[k] + lax.broadcasted_iota(jnp.int32, (tc, tm), 1)).astype(F32)
        p = jnp.where(cols == rank, 1.0, 0.0).astype(BF16)
        o_ref[...] += jnp.dot(p, ys_ref[...], preferred_element_type=F32)


def _moe_combine(h, rank_col, ys, items, ni_max, tc, tm):
    n, d = h.shape
    e = rank_col.shape[1]
    spec = pltpu.PrefetchScalarGridSpec(
        num_scalar_prefetch=6,
        grid=(ni_max,),
        in_specs=[pl.BlockSpec((tc, d), lambda k, t, c, *_: (c[k], 0)),
                  pl.BlockSpec((tc, e), lambda k, t, c, *_: (c[k], 0)),
                  pl.BlockSpec((tm, d), lambda k, t, c, *_: (t[k], 0))],
        out_specs=pl.BlockSpec((tc, d), lambda k, t, c, *_: (c[k], 0)),
    )
    return pl.pallas_call(
        _moe_combine_kernel,
        grid_spec=spec,
        out_shape=jax.ShapeDtypeStruct((n, d), F32),
        compiler_params=_params(("arbitrary",)),
        name="moe_combine",
    )(*items, h, rank_col, ys)


def _moe_grouped(h, nw, rwt, wgu, wd, tc=512, tm=512, tr=256):
    gates, u = _router(h, nw, rwt)
    rank_all = _moe_rank(gates, tc)
    tiles, by_tile, by_chunk, nt_max, ni_max = _moe_plan(rank_all, gates, tc, tm, tr)
    xs, gs = _moe_gather(u, rank_all, gates, by_tile, nt_max * tm, ni_max, tc, tr)
    ys = _moe_ffn(xs, gs, tiles, wgu, wd, nt_max, tm)
    rank_col = jnp.where(gates > 0.0, rank_all, -1.0).T
    return _moe_combine(h, rank_col, ys, by_chunk, ni_max, tc, tr)


def kernel(x, a_norm_w, a_w_in, a_q_norm_w, a_k_norm_w, a_w_out, kv_norm_w, kv_w, b_norm_w, b_w_q, b_w_out,
           ffn_norm_w, dense_w_gate_up, dense_w_down, moe_router_w, moe_w_gate_up, moe_w_down):
    b, s, d = x.shape
    n = b * s
    topk = min(MAX_TOPK, s // 4)
    tq_a, ch_a, tq_b = 256, 256, 256
    h = x.reshape(n, d)

    w_in = jnp.pad(a_w_in[0], ((0, 0), (0, A_IN_PAD - A_IN))).astype(BF16)
    proj = _norm_matmul(h, a_norm_w[0], w_in, F32)
    q, k, vt, qit, ki, wt = _dsa_prep(proj, a_q_norm_w[0], a_k_norm_w[0], b, s, tq_a, ch_a)
    o = _dsa_attn(qit, wt, ki, q, k, vt, b, s, tq_a, ch_a, topk)
    h = _matmul_residual(o, a_w_out[0].astype(BF16), h)
    h = _ffn_dense(h, ffn_norm_w[0], dense_w_gate_up[0].astype(BF16), dense_w_down[0].astype(BF16))

    kv = _norm_matmul(h, kv_norm_w, kv_w.astype(BF16), BF16)
    vt_sb = _transpose_chunks(kv, d, 1, b, s, tq_b)

    q_sb = _norm_matmul(h, b_norm_w[0], (b_w_q[0] * (HEAD_DIM ** -0.5 * LOG2E)).astype(BF16), BF16)
    o = _sb_attn(q_sb, kv, vt_sb, b, s, tq_b)
    h = _matmul_residual(o, b_w_out[0].astype(BF16), h)
    h = _moe_grouped(h, ffn_norm_w[1], moe_router_w[0].T,
                     moe_w_gate_up[0].astype(BF16), moe_w_down[0].astype(BF16))
    return h.reshape(b, s, d)
```

```python
import functools

import jax
import jax.numpy as jnp
from jax import lax
from jax.experimental import pallas as pl
from jax.experimental.pallas import tpu as pltpu

D_MODEL = 1024
HEAD_DIM = 128
N_HEADS_A = 8
N_KV_HEADS_A = 2
GROUP_A = 4
N_IDX_HEADS = 8
IDX_DIM = 64
MAX_TOPK = 256
N_HEADS_B = 8
ROPE_THETA = 500000.0
ROT_FRAC = 4
NORM_EPS = 1e-6
D_FF = 2816
N_EXPERTS = 8

A_Q = N_HEADS_A * HEAD_DIM
A_K = N_KV_HEADS_A * HEAD_DIM
A_V = N_KV_HEADS_A * HEAD_DIM
A_QI = N_IDX_HEADS * IDX_DIM
A_IN = A_Q + A_K + A_V + A_QI + IDX_DIM + N_IDX_HEADS
LANE = 128
SUBLANES = 8
F32_LOWEST = -3.4028234663852886e38
FIX_CAP = 64
MXU_DIM = 256
A_IN_PAD = A_Q + A_K + A_V + A_QI + LANE

DSA_TQ = MXU_DIM
DSA_CH = MXU_DIM
SB_TQ = MXU_DIM
SB_HEADS = 4
ROW_BLOCK = 512
FF_CHUNK = D_FF // 2
MOE_CHUNK = 512
MOE_TILE = 512
MOE_PERM_TILE = 256
INT_MIN = -(2 ** 31)
HALF_MIN = -(2 ** 15)
HALF_BITS = 16
HALF_SPAN = 2 ** HALF_BITS
ROT_HALF = HEAD_DIM // ROT_FRAC // 2
ROT_HALF_IDX = IDX_DIM // ROT_FRAC // 2
HALF_ROWS = 16
VT_ROWS = HEAD_DIM + HALF_ROWS
LOG2E = 1.4426950408889634
SB_DEAD_LOG2 = 151.0
VMEM_LIMIT = 56 * 1024 * 1024

F32 = jnp.float32
BF16 = jnp.bfloat16
NT_DIMS = (((1,), (1,)), ((), ()))


def _params(sem):
    return pltpu.CompilerParams(dimension_semantics=sem, vmem_limit_bytes=VMEM_LIMIT)


def _rms(x, w):
    ms = jnp.mean(x * x, axis=-1, keepdims=True)
    return x * lax.rsqrt(ms + NORM_EPS) * w


def _sb_proj_kernel(x_ref, kvnw_ref, qnw_ref, wkv_ref, wq_ref, k_ref, vt_ref, q_ref):
    x = x_ref[...]
    r = x * lax.rsqrt(jnp.mean(x * x, axis=-1, keepdims=True) + NORM_EPS)
    d = k_ref.shape[1]
    kv = jnp.dot((r * kvnw_ref[...]).astype(BF16), wkv_ref[...], preferred_element_type=F32)
    k_ref[...] = kv[:, :d].astype(k_ref.dtype)
    vt_ref[0, 0] = kv[:, d:].T.astype(vt_ref.dtype)
    q_ref[...] = jnp.dot((r * qnw_ref[...]).astype(BF16), wq_ref[...],
                         preferred_element_type=F32).astype(q_ref.dtype)


def _sb_proj(x, kv_nw, q_nw, wkv, wq, b, s, ch):
    n, d = x.shape
    nb = s // ch
    row = pl.BlockSpec((ch, d), lambda i: (i, 0))
    vec = pl.BlockSpec((1, d), lambda i: (0, 0))
    return pl.pallas_call(
        _sb_proj_kernel,
        grid=(n // ch,),
        in_specs=[row, vec, vec,
                  pl.BlockSpec(wkv.shape, lambda i: (0, 0)),
                  pl.BlockSpec(wq.shape, lambda i: (0, 0))],
        out_specs=(row, pl.BlockSpec((1, 1, d, ch), lambda i: (i // nb, i % nb, 0, 0)), row),
        out_shape=(jax.ShapeDtypeStruct((n, d), BF16),
                   jax.ShapeDtypeStruct((b, nb, d, ch), BF16),
                   jax.ShapeDtypeStruct((n, d), BF16)),
        compiler_params=_params(("parallel",)),
        name="sb_proj",
    )(x, kv_nw.reshape(1, d), q_nw.reshape(1, d), wkv, wq)


def _rope(x, c, s_lo, s_hi, half):
    n = x.shape[-1]
    return x * c + pltpu.roll(x, n - half, 1) * s_lo + pltpu.roll(x, half, 1) * s_hi


def _dsa_prep_kernel(x_ref, nw_ref, w_ref, qw_ref, kw_ref, c_ref, slo_ref, shi_ref, ci_ref, sloi_ref, shii_ref,
                     q_ref, k_ref, vt_ref, qit_ref, ki_ref, wt_ref, p_ref, *, tq):
    bm = p_ref.shape[0]
    p_ref[...] = jnp.dot(_rms(x_ref[...], nw_ref[...]).astype(BF16), w_ref[...], preferred_element_type=F32)
    c, slo, shi = c_ref[...], slo_ref[...], shi_ref[...]
    ci, sloi, shii = ci_ref[...], sloi_ref[...], shii_ref[...]
    scale = HEAD_DIM ** -0.5 * LOG2E
    for h in range(N_HEADS_A):
        xh = _rms(p_ref[:, h * HEAD_DIM:(h + 1) * HEAD_DIM], qw_ref[...])
        q_ref[:, h * HEAD_DIM:(h + 1) * HEAD_DIM] = (_rope(xh, c, slo, shi, ROT_HALF) * scale).astype(BF16)
    for h in range(N_KV_HEADS_A):
        xh = _rms(p_ref[:, A_Q + h * HEAD_DIM:A_Q + (h + 1) * HEAD_DIM], kw_ref[...])
        k_ref[:, h * HEAD_DIM:(h + 1) * HEAD_DIM] = _rope(xh, c, slo, shi, ROT_HALF).astype(BF16)
    ones = jnp.ones((VT_ROWS - HEAD_DIM, bm), BF16)
    for h in range(N_KV_HEADS_A):
        v = p_ref[:, A_Q + A_K + h * HEAD_DIM:A_Q + A_K + (h + 1) * HEAD_DIM]
        vt_ref[0, 0, h * VT_ROWS:h * VT_ROWS + HEAD_DIM, :] = v.T.astype(BF16)
        vt_ref[0, 0, h * VT_ROWS + HEAD_DIM:(h + 1) * VT_ROWS, :] = ones
    off = A_Q + A_K + A_V
    for j in range(A_QI // LANE):
        xj = _rope(p_ref[:, off + j * LANE:off + (j + 1) * LANE], ci, sloi, shii, ROT_HALF_IDX)
        xt =xj.T.astype(BF16)
        for b in range(bm // tq):
            qit_ref[0, b, j * LANE:(j + 1) * LANE, :] = xt[:, b * tq:(b + 1) * tq]
    off += A_QI
    lane = lax.broadcasted_iota(jnp.int32, (1, LANE), 1)
    first = lane < IDX_DIM
    kw = p_ref[:, off:off + LANE]
    kw = _rope(kw, jnp.where(first, ci, 1.0), jnp.where(first, sloi, 0.0), jnp.where(first, shii, 0.0),
               ROT_HALF_IDX)
    ki_ref[...] = kw[:, :IDX_DIM].astype(BF16)
    wt = kw.T[IDX_DIM:IDX_DIM + N_IDX_HEADS, :] * (N_IDX_HEADS ** -0.5) * (IDX_DIM ** -0.5)
    for b in range(bm // tq):
        wt_ref[0, b] = wt[:, b * tq:(b + 1) * tq]


def _rope_tables(s, head, reps):
    r = head // ROT_FRAC
    half = r // 2
    inv = ROPE_THETA ** (-jnp.arange(half, dtype=F32) * 2.0 / r)
    ang = jnp.arange(s).astype(F32)[:, None] * inv[None, :]
    cos, sin = jnp.cos(ang), jnp.sin(ang)
    one = jnp.ones((s, head - r), F32)
    zero = jnp.zeros((s, head - r), F32)
    zh = jnp.zeros((s, half), F32)
    c = jnp.concatenate([cos, cos, one], axis=1)
    s_lo = jnp.concatenate([-sin, zh, zero], axis=1)
    s_hi = jnp.concatenate([zh, sin, zero], axis=1)
    return tuple(jnp.tile(t, (1, reps)) for t in (c, s_lo, s_hi))


def _dsa_prep(x, nw, w_in, qw, kw, b, s, tq, ch):
    n, d = x.shape
    bm = ch
    nb = s // bm
    c, slo, shi = _rope_tables(s, HEAD_DIM, 1)
    ci, sloi, shii = _rope_tables(s, IDX_DIM, 2)
    tab = pl.BlockSpec((bm, LANE), lambda i: (i % nb, 0))
    row = lambda w: pl.BlockSpec((bm, w), lambda i: (i, 0))
    out_shapes = (
        jax.ShapeDtypeStruct((n, A_Q), BF16),
        jax.ShapeDtypeStruct((n, A_K), BF16),
        jax.ShapeDtypeStruct((b, s // ch, N_KV_HEADS_A * VT_ROWS, ch), BF16),
        jax.ShapeDtypeStruct((b, s // tq, A_QI, tq), BF16),
        jax.ShapeDtypeStruct((n, IDX_DIM), BF16),
        jax.ShapeDtypeStruct((b, s // tq, N_IDX_HEADS, tq), F32),
    )
    out_specs = (
        row(A_Q), row(A_K),
        pl.BlockSpec((1, 1, N_KV_HEADS_A * VT_ROWS, ch), lambda i: (i // nb, i % nb, 0, 0)),
        pl.BlockSpec((1, bm // tq, A_QI, tq), lambda i: (i // nb, i % nb, 0, 0)),
        row(IDX_DIM),
        pl.BlockSpec((1, bm // tq, N_IDX_HEADS, tq), lambda i: (i // nb, i % nb, 0, 0)),
    )
    return pl.pallas_call(
        functools.partial(_dsa_prep_kernel, tq=tq),
        grid=(n // bm,),
        in_specs=[row(d),
                  pl.BlockSpec((1, d), lambda i: (0, 0)),
                  pl.BlockSpec(w_in.shape, lambda i: (0, 0)),
                  pl.BlockSpec((1, HEAD_DIM), lambda i: (0, 0)),
                  pl.BlockSpec((1, HEAD_DIM), lambda i: (0, 0)),
                  tab, tab, tab, tab, tab, tab],
        out_specs=out_specs,
        out_shape=out_shapes,
        scratch_shapes=[pltpu.VMEM((bm, A_IN_PAD), F32)],
        compiler_params=_params(("parallel",)),
        name="dsa_prep",
    )(x, nw.reshape(1, d), w_in, qw.reshape(1, -1), kw.reshape(1, -1), c, slo, shi, ci, sloi, shii)


def _dsa_attn_kernel(qit_ref, wt_ref, ki_ref, q_ref, k_ref, vt_ref, o_ref,
                     sc_ref, hi_ref, lo_ref, thrf_ref, cnt_ref, acc_ref, *, tq, ch, topk):
    i = pl.program_id(1)
    nch = ((i + 1) * tq + ch - 1) // ch
    t_row = i * tq + lax.broadcasted_iota(jnp.int32, (1, tq), 1)

    qit = jnp.concatenate([qit_ref[0, 0, h * IDX_DIM:(h + 1) * IDX_DIM, :] for h in range(N_IDX_HEADS)],
                          axis=1)
    wt = wt_ref[0, 0]

    def to_key(x):
        bits = lax.bitcast_convert_type(x, jnp.int32)
        return bits ^ ((bits >> 31) & 0x7FFFFFFF)

    npair = (nch + 1) // 2

    def score_body(c2, carry):
        starts = [pl.multiple_of(c2 * (2 * ch) + half * ch, ch) for half in range(2)]
        rels = [jnp.dot(ki_ref[pl.ds(s0, ch), :], qit, preferred_element_type=F32) for s0 in starts]
        for s0, rel in zip(starts, rels):
            sc = jnp.zeros((ch, tq), F32)
            for h in range(N_IDX_HEADS):
                sc = sc + jnp.maximum(rel[:, h * tq:(h + 1) * tq], 0.0) * wt[h:h + 1, :]
            sc = jnp.where(sc == 0.0, 0.0, sc)
            causal = (s0 + lax.broadcasted_iota(jnp.int32, (ch, tq), 0)) <= t_row
            sc_ref[pl.ds(s0, ch), :] = jnp.where(causal, sc, -jnp.inf)
            key = jnp.where(causal, to_key(sc), INT_MIN)
            hi_ref[pl.ds(s0, ch), :] = (key >> HALF_BITS).astype(jnp.int16)
        return carry

    lax.fori_loop(0, npair, score_body, 0)

    def count16(ref, thr):
        t16 = jnp.broadcast_to(thr.astype(jnp.int16), (HALF_ROWS, tq))

        def body(c2, acc):
            s0 = pl.multiple_of(c2 * (2 * ch), 2 * ch)
            x = ref[pl.ds(s0, 2 * ch), :].reshape(2 * ch // HALF_ROWS, HALF_ROWS, tq)
            hit = jnp.where(x >= t16[None], jnp.int16(1), jnp.int16(0))
            parts = [hit[r] for r in range(2 * ch // HALF_ROWS)]
            while len(parts) > 1:
                parts = [a + b for a, b in zip(parts[0::2], parts[1::2])]
            return acc + parts[0]

        acc = lax.fori_loop(0, npair, body, jnp.zeros((HALF_ROWS, tq), jnp.int16))
        return jnp.sum(acc.astype(jnp.int32), axis=0, keepdims=True)

    def bisect_pass(ref, want, bit, st):
        thr, cnt, over = st
        cand = thr + (jnp.int32(1) << bit)
        c = count16(ref, cand)
        take = c >= want
        return jnp.where(take, cand, thr), jnp.where(take, c, cnt), jnp.where(take, over, c)

    def bisect_start(ref, want, cnt_floor):
        c0 = count16(ref, jnp.zeros((1, tq), jnp.int32))
        pos = c0 >= want
        return jnp.where(pos, 0, HALF_MIN), jnp.where(pos, c0, cnt_floor), jnp.where(pos, 0, c0)

    short = t_row < topk
    live = jnp.logical_not(short)
    want_hi = jnp.full((1, tq), topk, jnp.int32)
    st = bisect_start(hi_ref, want_hi, jnp.full((1, tq), npair * 2 * ch, jnp.int32))
    thr_hi, cnt_hi, above = lax.fori_loop(0, 15, lambda it, s: bisect_pass(hi_ref, want_hi, 14 - it, s), st)
    want_lo = topk - above

    def lo_body(c2, carry):
        s0 = pl.multiple_of(c2 * (2 * ch), 2 * ch)
        key = to_key(sc_ref[pl.ds(s0, 2 * ch), :])
        lo = (key & (HALF_SPAN - 1)) + HALF_MIN
        lo_ref[pl.ds(s0, 2 * ch), :] = jnp.where((key >> HALF_BITS) == thr_hi, lo, HALF_MIN).astype(jnp.int16)
        return carry

    lax.fori_loop(0, npair, lo_body, 0)

    def n_open(cnt):
        return jnp.sum(jnp.where(jnp.logical_and(live, cnt != want_lo), 1, 0))

    def lo_cond(s):
        return jnp.logical_and(s[0] < 15, s[2] > 0)

    def lo_step(s):
        it, st, _ = s
        for k in range(3):
            st = bisect_pass(lo_ref, want_lo, 14 - it - k, st)
        return it + 3, st, n_open(st[1])

    st = bisect_start(lo_ref, want_lo, cnt_hi - above)
    _, (thr_lo, _, _), _ = lax.while_loop(lo_cond, lo_step, (jnp.int32(0), st, n_open(st[1])))
    thr_key = thr_hi * HALF_SPAN + (thr_lo - HALF_MIN)
    thr0 = lax.bitcast_convert_type(thr_key ^ ((thr_key >> 31) & 0x7FFFFFFF), F32)

    def sweep(pick, init, fold, join):
        def body(c2, acc):
            s0 = pl.multiple_of(c2 * (2 * ch), 2 * ch)
            v = pick(sc_ref[pl.ds(s0, 2 * ch), :]).reshape(2 * ch // SUBLANES, SUBLANES, tq)
            return join(acc, fold(v, axis=0))
        acc = lax.fori_loop(0, npair, body, jnp.full((SUBLANES, tq), init))
        return fold(acc, axis=0, keepdims=True)

    def count_where(cond):
        return sweep(lambda x: jnp.where(cond(x), 1, 0), jnp.int32(0), jnp.sum, jnp.add)

    def lowest_where(cond):
        return sweep(lambda x: jnp.where(cond(x), x, jnp.inf), jnp.float32(jnp.inf), jnp.min, jnp.minimum)

    thr0 = jnp.where(short, F32_LOWEST, thr0)
    thrf_ref[...] = thr0
    cnt_ref[...] = count_where(lambda x: x >= thr0)

    def n_rows(cond):
        return jnp.sum(jnp.where(jnp.logical_and(live, cond(cnt_ref[...])), 1, 0))

    def fix_cond(s):
        return jnp.logical_and(s[0] < FIX_CAP, s[1] > 0)

    @pl.when(n_rows(lambda c: c != topk) > 0)
    def _():
        def grow(s):
            thr, cnt = thrf_ref[...], cnt_ref[...]
            nxt = sweep(lambda x: jnp.where(x < thr, x, -jnp.inf), jnp.float32(-jnp.inf), jnp.max, jnp.maximum)
            move = jnp.logical_and(jnp.logical_and(live, cnt < topk), nxt > -jnp.inf)
            thr = jnp.where(move, nxt, thr)
            thrf_ref[...] = thr
            cnt_ref[...] = jnp.where(move, count_where(lambda x: x >= thr), cnt)
            return s[0] + 1, n_rows(lambda c: c < topk)

        lax.while_loop(fix_cond, grow, (jnp.int32(0), n_rows(lambda c: c < topk)))

        def shrink(s):
            thr, cnt = thrf_ref[...], cnt_ref[...]
            many = jnp.logical_and(live, cnt > topk)
            cur = lowest_where(lambda x: x >= thr)
            same = count_where(lambda x: x == cur)
            drop = jnp.logical_and(many, cnt - same >= topk)
            nxt = lowest_where(lambda x: x > cur)
            thrf_ref[...] = jnp.where(drop, nxt, jnp.where(many, cur, thr))
            cnt_ref[...] = jnp.where(drop, cnt - same, cnt)
            return s[0] + 1, jnp.sum(jnp.where(drop, 1, 0))

        lax.while_loop(fix_cond, shrink, (jnp.int32(0), n_rows(lambda c: c > topk)))

    @pl.when(n_rows(lambda c: c > topk) > 0)
    def _():
        tied = jnp.logical_and(live, cnt_ref[...] > topk)
        thr = jnp.where(tied, thrf_ref[...], jnp.inf)
        need = (topk - count_where(lambda x: x > thr)).astype(F32)
        r = lax.broadcasted_iota(jnp.int32, (ch, ch), 0)
        cc = lax.broadcasted_iota(jnp.int32, (ch, ch), 1)
        tri = jnp.where(cc <= r, 1.0, 0.0).astype(BF16)

        def tie_body(c, run):
            s0 = pl.multiple_of(c * ch, ch)
            x = sc_ref[pl.ds(s0, ch), :]
            eq = x == thr
            eqf = jnp.where(eq, 1.0, 0.0)
            pre = jnp.dot(tri, eqf.astype(BF16), preferred_element_type=F32) + run
            sc_ref[pl.ds(s0, ch), :] = jnp.where(jnp.logical_and(eq, pre > need), -jnp.inf, x)
            return run + jnp.sum(eqf, axis=0, keepdims=True)

        lax.fori_loop(0, nch, tie_body, jnp.zeros((1, tq), F32))

    thr = thrf_ref[...]

    heads = range(N_HEADS_A)
    acc_ref[...] = jnp.zeros(acc_ref.shape, F32)

    def att_body(c2, carry):
        s0 = pl.multiple_of(c2 * (2 * ch), 2 * ch)
        bias = jnp.where(sc_ref[pl.ds(s0, 2 * ch), :] >= thr, 0.0, -jnp.inf)
        lgs = [lax.dot_general(k_ref[pl.ds(s0, 2 * ch), (h // GROUP_A) * HEAD_DIM:(h // GROUP_A + 1) * HEAD_DIM],
                               q_ref[:, h * HEAD_DIM:(h + 1) * HEAD_DIM], NT_DIMS,
                               preferred_element_type=F32) for h in heads]
        lgs = [(lg + bias).astype(BF16) for lg in lgs]
        m_new = [jnp.maximum(carry[h], jnp.max(lgs[h], axis=0, keepdims=True).astype(F32)) for h in heads]
        m_safe = [jnp.where(m == -jnp.inf, 0.0, m) for m in m_new]
        ps = [jnp.exp2(lgs[h] - m_safe[h].astype(BF16)) for h in heads]
        pvs = [sum(jnp.dot(vt_ref[0, 2 * c2 + half, (h // GROUP_A) * VT_ROWS:(h // GROUP_A + 1) * VT_ROWS, :],
                           ps[h][half * ch:(half + 1) * ch], preferred_element_type=F32) for half in range(2))
               for h in heads]
        for h in heads:
            acc_ref[h] = acc_ref[h] * jnp.exp2(carry[h] - m_safe[h]) + pvs[h]
        return tuple(m_new)

    m0 = jnp.full((1, tq), -jnp.inf, F32)
    lax.fori_loop(0, npair, att_body, (m0,) * N_HEADS_A)
    for h in heads:
        og = acc_ref[h, :HEAD_DIM, :] / acc_ref[h, HEAD_DIM:HEAD_DIM + 1, :]
        o_ref[:, h * HEAD_DIM:(h + 1) * HEAD_DIM] = og.T.astype(o_ref.dtype)


def _dsa_attn(qit, wt, ki, q, k, vt, b, s, tq, ch, topk):
    nq = s // tq
    return pl.pallas_call(
        functools.partial(_dsa_attn_kernel, tq=tq, ch=ch, topk=topk),
        grid=(b, nq),
        in_specs=[pl.BlockSpec((1, 1, A_QI, tq), lambda bb, i: (bb, i, 0, 0)),
                  pl.BlockSpec((1, 1, N_IDX_HEADS, tq), lambda bb, i: (bb, i, 0, 0)),
                  pl.BlockSpec((s, IDX_DIM), lambda bb, i: (bb, 0)),
                  pl.BlockSpec((tq, A_Q), lambda bb, i: (bb * nq + i, 0)),
                  pl.BlockSpec((s, A_K), lambda bb, i: (bb, 0)),
                  pl.BlockSpec((1, s // ch, N_KV_HEADS_A * VT_ROWS, ch), lambda bb, i: (bb, 0, 0, 0))],
        out_specs=pl.BlockSpec((tq, A_Q), lambda bb, i: (bb * nq + i, 0)),
        out_shape=jax.ShapeDtypeStruct((b * s, A_Q), BF16),
        scratch_shapes=[pltpu.VMEM((s, tq), F32),
                        pltpu.VMEM((s, tq), jnp.int16),
                        pltpu.VMEM((s, tq), jnp.int16),
                        pltpu.VMEM((1, tq), F32),
                        pltpu.VMEM((1, tq), jnp.int32),
                        pltpu.VMEM((N_HEADS_A, VT_ROWS, tq), F32)],
        compiler_params=_params(("parallel", "arbitrary")),
        name="dsa_attn",
    )(qit, wt, ki, q, k, vt)


def _sb_attn_kernel(q_ref, k_ref, vt_ref, o_ref, acc_ref, *, tq, hb):
    i = pl.program_id(2)
    ch = tq
    t_row = i * tq + lax.broadcasted_iota(jnp.int32, (1, tq), 1)
    r = lax.broadcasted_iota(jnp.int32, (ch, ch), 0)
    cc = lax.broadcasted_iota(jnp.int32, (ch, ch), 1)
    tri = jnp.where(cc >= r, 1.0, 0.0).astype(BF16)
    acc_ref[...] = jnp.zeros(acc_ref.shape, F32)

    def cond(st):
        j, _, low = st
        return jnp.logical_and(j <= i, low <= SB_DEAD_LOG2)

    def body(st):
        j, runs, _ = st
        c = i - j
        s0 = pl.multiple_of(c * ch, ch)
        strict = (s0 + lax.broadcasted_iota(jnp.int32, (ch, tq), 0)) < t_row
        heads = [slice(h * HEAD_DIM, (h + 1) * HEAD_DIM) for h in range(hb)]
        zs = [lax.dot_general(k_ref[pl.ds(s0, ch), hs], q_ref[:, hs], NT_DIMS, preferred_element_type=F32)
              for hs in heads]
        sps = [jnp.where(strict, jnp.maximum(z, 0.0) + jnp.log(1.0 + jnp.exp2(-jnp.abs(z))) * LOG2E, 0.0)
               for z in zs]
        his = [sp.astype(BF16) for sp in sps]
        los = [(sp - hi.astype(F32)).astype(BF16) for sp, hi in zip(sps, his)]
        cums = [jnp.dot(tri, hi, preferred_element_type=F32) + jnp.dot(tri, lo, preferred_element_type=F32) + run
                for hi, lo, run in zip(his, los, runs)]
        aas = [jnp.where(strict, jnp.exp2(z - cum), 0.0).astype(BF16) for z, cum in zip(zs, cums)]
        pvs = [jnp.dot(vt_ref[0, c, hs, :], a, preferred_element_type=F32) for hs, a in zip(heads, aas)]
        new_runs = [run + jnp.sum(sp, axis=0, keepdims=True) for run, sp in zip(runs, sps)]
        for h in range(hb):
            acc_ref[h] += pvs[h]
        low = jnp.min(functools.reduce(jnp.minimum, new_runs))
        return j + 1, tuple(new_runs), low

    run0 = tuple(jnp.zeros((1, tq), F32) for _ in range(hb))
    lax.while_loop(cond, body, (jnp.int32(0), run0, jnp.float32(0.0)))
    for h in range(hb):
        o_ref[:, h * HEAD_DIM:(h + 1) * HEAD_DIM] = acc_ref[h].T.astype(o_ref.dtype)


def _sb_attn(q, k, vt, b, s, tq, hb=SB_HEADS):
    nq = s // tq
    d = N_HEADS_B * HEAD_DIM
    w = hb * HEAD_DIM
    return pl.pallas_call(
        functools.partial(_sb_attn_kernel, tq=tq, hb=hb),
        grid=(b, N_HEADS_B // hb, nq),
        in_specs=[pl.BlockSpec((tq, w), lambda bb, h, i: (bb * nq + i, h)),
                  pl.BlockSpec((s, w), lambda bb, h, i: (bb, h)),
                  pl.BlockSpec((1, s // tq, w, tq), lambda bb, h, i: (bb, 0, h, 0))],
        out_specs=pl.BlockSpec((tq, w), lambda bb, h, i: (bb * nq + i, h)),
        out_shape=jax.ShapeDtypeStruct((b * s, d), BF16),
        scratch_shapes=[pltpu.VMEM((hb, HEAD_DIM, tq), F32)],
        compiler_params=_params(("parallel", "parallel", "arbitrary")),
        name="sb_attn",
    )(q, k, vt)


def _silu_mul(g, up):
    return g * (1.0 / (1.0 + jnp.exp(-g))) * up


def _ffn_kernel(a_ref, wo_ref, x_ref, nw_ref, wg_ref, wu_ref, wd_ref, o_ref, u_ref):
    j = pl.program_id(1)

    @pl.when(j == 0)
    def _():
        x = x_ref[...] + jnp.dot(a_ref[...], wo_ref[...], preferred_element_type=F32)
        u_ref[...] = _rms(x, nw_ref[...]).astype(BF16)
        o_ref[...] = x

    u = u_ref[...]
    g = jnp.dot(u, wg_ref[...], preferred_element_type=F32)
    up = jnp.dot(u, wu_ref[...], preferred_element_type=F32)
    o_ref[...] += jnp.dot(_silu_mul(g, up).astype(BF16), wd_ref[...], preferred_element_type=F32)


def _ffn_dense(a, wo, x, nw, wgu, wd, bm=ROW_BLOCK, fc=FF_CHUNK):
    n, d = x.shape
    nj = D_FF // fc
    return pl.pallas_call(
        _ffn_kernel,
        grid=(n // bm, nj),
        in_specs=[pl.BlockSpec((bm, a.shape[1]), lambda i, j: (i, 0)),
                  pl.BlockSpec(wo.shape, lambda i, j: (0, 0)),
                  pl.BlockSpec((bm, d), lambda i, j: (i, 0)),
                  pl.BlockSpec((1, d), lambda i, j: (0, 0)),
                  pl.BlockSpec((d, fc), lambda i, j: (0, j)),
                  pl.BlockSpec((d, fc), lambda i, j: (0, j + nj)),
                  pl.BlockSpec((fc, d), lambda i, j: (j, 0))],
        out_specs=pl.BlockSpec((bm, d), lambda i, j: (i, 0)),
        out_shape=jax.ShapeDtypeStruct((n, d), F32),
        scratch_shapes=[pltpu.VMEM((bm, d), BF16)],
        compiler_params=_params(("parallel", "arbitrary")),
        name="ffn_dense",
    )(a, wo, x, nw.reshape(1, d), wgu, wgu, wd)


def _router_kernel(a_ref, wo_ref, x_ref, nw_ref, rwt_ref, g_ref, u_ref, h_ref):
    h = x_ref[...] + jnp.dot(a_ref[...], wo_ref[...], preferred_element_type=F32)
    h_ref[...] = h
    u = _rms(h, nw_ref[...])
    u_ref[...] = u.astype(BF16)
    lt = lax.dot_general(rwt_ref[...], u, NT_DIMS, preferred_element_type=F32,
                         precision=lax.Precision.HIGHEST)
    e = lax.broadcasted_iota(jnp.int32, lt.shape, 0)
    m1 = jnp.max(lt, axis=0, keepdims=True)
    i1 = jnp.min(jnp.where(lt == m1, e, N_EXPERTS), axis=0, keepdims=True)
    rest = jnp.where(e == i1, -jnp.inf, lt)
    m2 = jnp.max(rest, axis=0, keepdims=True)
    i2 = jnp.min(jnp.where(rest == m2, e, N_EXPERTS), axis=0, keepdims=True)
    w2 = jnp.exp(m2 - m1)
    den = 1.0 + w2
    g_ref[...] = jnp.where(e == i1, 1.0 / den, 0.0) + jnp.where(e == i2, w2 / den, 0.0)


def _router(a, wo, x, nw, rwt, bm=ROW_BLOCK):
    n, d = x.shape
    return pl.pallas_call(
        _router_kernel,
        grid=(n // bm,),
        in_specs=[pl.BlockSpec((bm, a.shape[1]), lambda i: (i, 0)),
                  pl.BlockSpec(wo.shape, lambda i: (0, 0)),
                  pl.BlockSpec((bm, d), lambda i: (i, 0)),
                  pl.BlockSpec((1, d), lambda i: (0, 0)),
                  pl.BlockSpec((N_EXPERTS, d), lambda i: (0, 0))],
        out_specs=(pl.BlockSpec((N_EXPERTS, bm), lambda i: (0, i)),
                   pl.BlockSpec((bm, d), lambda i: (i, 0)),
                   pl.BlockSpec((bm, d), lambda i: (i, 0))),
        out_shape=(jax.ShapeDtypeStruct((N_EXPERTS, n), F32),
                   jax.ShapeDtypeStruct((n, d), BF16),
                   jax.ShapeDtypeStruct((n, d), F32)),
        compiler_params=_params(("parallel",)),
        name="moe_router",
    )(a, wo, x, nw.reshape(1, d), rwt)


def _moe_rank_kernel(g_ref, rank_ref, carry_ref):
    @pl.when(pl.program_id(0) == 0)
    def _():
        carry_ref[...] = jnp.zeros(carry_ref.shape, F32)

    tc = g_ref.shape[1]
    m = jnp.where(g_ref[...] > 0.0, 1.0, 0.0)
    r = lax.broadcasted_iota(jnp.int32, (tc, tc), 0)
    c = lax.broadcasted_iota(jnp.int32, (tc, tc), 1)
    before = jnp.where(r < c, 1.0, 0.0).astype(BF16)
    rank_ref[...] = jnp.dot(m.astype(BF16), before, preferred_element_type=F32) + carry_ref[:, :1]
    carry_ref[...] = carry_ref[...] + jnp.sum(m, axis=1, keepdims=True)


def _moe_rank(gates, tc):
    e, n = gates.shape
    return pl.pallas_call(
        _moe_rank_kernel,
        grid=(n // tc,),
        in_specs=[pl.BlockSpec((e, tc), lambda i: (0, i))],
        out_specs=pl.BlockSpec((e, tc), lambda i: (0, i)),
        out_shape=jax.ShapeDtypeStruct((e, n), F32),
        scratch_shapes=[pltpu.VMEM((e, LANE), F32)],
        compiler_params=_params(("arbitrary",)),
        name="moe_rank",
    )(gates)


def _moe_plan(rank_all, gates, tc, tm, tr):
    e_n, n = rank_all.shape
    nc = n // tc
    nt_max = 2 * n // tm + e_n
    nr_max = nt_max * (tm // tr)
    ni_max = e_n * nc + nr_max
    slots = tc // tr + 1
    i32 = jnp.int32
    cstart = rank_all[:, ::tc].astype(i32)
    counts = (rank_all[:, -1] + jnp.where(gates[:, -1] > 0.0, 1.0, 0.0)).astype(i32)
    cend = jnp.concatenate([cstart[:, 1:], counts[:, None]], axis=1)
    tiles = (counts + tm - 1) // tm
    tend = jnp.cumsum(tiles)
    tstart = tend - tiles
    total = tend[-1]
    t_idx = jnp.arange(nt_max, dtype=i32)
    tvalid = t_idx < total
    texp = jnp.sum((t_idx[:, None] >= tend[None, :]).astype(i32), axis=1)
    last_exp = jnp.sum((total - 1 >= tend).astype(i32))
    texp = jnp.where(tvalid, texp, last_exp)
    tsrc = jnp.where(tvalid, t_idx, 0)

    nonempty = cend > cstart
    t0 = cstart // tr
    t1 = (cend - 1) // tr
    local = jnp.stack([t0 + j for j in range(slots)], axis=-1)
    ok = jnp.stack([nonempty & (t0 + j <= t1) for j in range(slots)], axis=-1).reshape(-1)
    shape = local.shape
    tile = (tstart[:, None, None] * (tm // tr) + local).reshape(-1)
    base = (local * tr).reshape(-1)
    exp = jnp.broadcast_to(jnp.arange(e_n, dtype=i32)[:, None, None], shape).reshape(-1)
    chunk = jnp.broadcast_to(jnp.arange(nc, dtype=i32)[None, :, None], shape).reshape(-1)
    n_items = jnp.sum(ok.astype(i32))
    k_idx = jnp.arange(ni_max, dtype=i32)
    valid = k_idx < n_items

    def ordered(key, group):
        order = jnp.argsort(jnp.where(ok, key, jnp.int32(2 ** 30)))[:ni_max]
        src = jnp.where(valid, order, order[n_items - 1])
        grp = group[src]
        first = jnp.concatenate([jnp.ones((1,), bool), grp[1:] != grp[:-1]]) & valid
        return (tile[src], chunk[src], exp[src], base[src], first.astype(i32), valid.astype(i32))

    by_tile = ordered(tile * nc + chunk, tile)
    by_chunk = ordered(chunk * nr_max + tile, chunk)
    return (texp, tvalid.astype(i32), tsrc), by_tile, by_chunk, nt_max, ni_max


def _moe_gather_kernel(tile_s, chunk_s, exp_s, base_s, first_s, valid_s, u_ref, rank_ref, g_ref,
                       x_zero_ref, gs_zero_ref, x_ref, gs_ref):
    del x_zero_ref, gs_zero_ref
    k = pl.program_id(0)
    tm, tc = x_ref.shape[0], u_ref.shape[0]

    @pl.when(valid_s[k] == 1)
    def _():
        e = exp_s[k]
        gate = g_ref[pl.ds(e, 1), :]
        rank = jnp.where(gate > 0.0, rank_ref[pl.ds(e, 1), :], -1.0)
        rows = (base_s[k] + lax.broadcasted_iota(jnp.int32, (tm, tc), 0)).astype(F32)
        p = jnp.where(rows == rank, 1.0, 0.0)
        xs = jnp.dot(p.astype(BF16), u_ref[...], preferred_element_type=F32).astype(BF16)
        gsel = jnp.sum(p * gate, axis=1, keepdims=True)

        @pl.when(first_s[k] == 1)
        def _():
            x_ref[...] = xs
            gs_ref[...] = gsel

        @pl.when(first_s[k] == 0)
        def _():
            x_ref[...] = x_ref[...] + xs
            gs_ref[...] = gs_ref[...] + gsel


def _moe_gather(u, rank_all, gates, items, n_rows, ni_max, tc, tr):
    n, d = u.shape
    e = gates.shape[0]
    n_prefetch = len(items)
    spec = pltpu.PrefetchScalarGridSpec(
        num_scalar_prefetch=n_prefetch,
        grid=(ni_max,),
        in_specs=[pl.BlockSpec((tc, d), lambda k, t, c, *_: (c[k], 0)),
                  pl.BlockSpec((e, tc), lambda k, t, c, *_: (0, c[k])),
                  pl.BlockSpec((e, tc), lambda k, t, c, *_: (0, c[k])),
                  pl.BlockSpec(memory_space=pl.ANY),
                  pl.BlockSpec(memory_space=pl.ANY)],
        out_specs=(pl.BlockSpec((tr, d), lambda k, t, c, *_: (t[k], 0)),
                   pl.BlockSpec((tr, 1), lambda k, t, c, *_: (t[k], 0))),
    )
    return pl.pallas_call(
        _moe_gather_kernel,
        grid_spec=spec,
        out_shape=(jax.ShapeDtypeStruct((n_rows, d), BF16),
                   jax.ShapeDtypeStruct((n_rows, 1), F32)),
        input_output_aliases={n_prefetch + 3: 0, n_prefetch + 4: 1},
        compiler_params=_params(("arbitrary",)),
        name="moe_gather",
    )(*items, u, rank_all, gates, jnp.zeros((n_rows, d), BF16), jnp.zeros((n_rows, 1), F32))


def _moe_ffn_kernel(texp_s, tvalid_s, tsrc_s, x_ref, gs_ref, wg_ref, wu_ref, wd_ref, o_ref, acc_ref):
    t = pl.program_id(0)
    j = pl.program_id(1)
    valid = tvalid_s[t] == 1

    @pl.when(valid)
    def _():
        x = x_ref[...]
        g = jnp.dot(x, wg_ref[0], preferred_element_type=F32)
        up = jnp.dot(x, wu_ref[0], preferred_element_type=F32)
        hm = (_silu_mul(g, up) * gs_ref[...]).astype(BF16)
        y = jnp.dot(hm, wd_ref[0], preferred_element_type=F32)

        @pl.when(j == 0)
        def _():
            acc_ref[...] = y

        @pl.when(j > 0)
        def _():
            acc_ref[...] += y

    @pl.when(j == pl.num_programs(1) - 1)
    def _():
        @pl.when(valid)
        def _():
            o_ref[...] = acc_ref[...].astype(o_ref.dtype)

        @pl.when(jnp.logical_not(valid))
        def _():
            o_ref[...] = jnp.zeros(o_ref.shape, o_ref.dtype)


def _moe_ffn(xs, gs, tiles, wgu, wd, nt_max, tm, fc=FF_CHUNK):
    d = xs.shape[1]
    nj = D_FF // fc
    spec = pltpu.PrefetchScalarGridSpec(
        num_scalar_prefetch=3,
        grid=(nt_max, nj),
        in_specs=[pl.BlockSpec((tm, d), lambda t, j, te, tv, ts: (ts[t], 0)),
                  pl.BlockSpec((tm, 1), lambda t, j, te, tv, ts: (ts[t], 0)),
                  pl.BlockSpec((1, d, fc), lambda t, j, te, tv, ts: (te[t], 0, j)),
                  pl.BlockSpec((1, d, fc), lambda t, j, te, tv, ts: (te[t], 0, j + nj)),
                  pl.BlockSpec((1, fc, d), lambda t, j, te, tv, ts: (te[t], j, 0))],
        out_specs=pl.BlockSpec((tm, d), lambda t, j, te, tv, ts: (t, 0)),
        scratch_shapes=[pltpu.VMEM((tm, d), F32)],
    )
    return pl.pallas_call(
        _moe_ffn_kernel,
        grid_spec=spec,
        out_shape=jax.ShapeDtypeStruct((nt_max * tm, d), BF16),
        compiler_params=_params(("arbitrary", "arbitrary")),
        name="moe_ffn",
    )(*tiles, xs, gs, wgu, wgu, wd)


def _moe_combine_kernel(tile_s, chunk_s, exp_s, base_s, first_s, valid_s, h_ref, rcol_ref, ys_ref, o_ref):
    k = pl.program_id(0)
    tc, tm = h_ref.shape[0], ys_ref.shape[0]

    @pl.when(first_s[k] == 1)
    def _():
        o_ref[...] = h_ref[...]

    @pl.when(valid_s[k] == 1)
    def _():
        lane = lax.broadcasted_iota(jnp.int32, rcol_ref.shape, 1)
        rank = jnp.sum(jnp.where(lane == exp_s[k], rcol_ref[...], 0.0), axis=1, keepdims=True)
        cols = (base_s[k] + lax.broadcasted_iota(jnp.int32, (tc, tm), 1)).astype(F32)
        p = jnp.where(cols == rank, 1.0, 0.0).astype(BF16)
        o_ref[...] += jnp.dot(p, ys_ref[...], preferred_element_type=F32)


def _moe_combine(h, rank_col, ys, items, ni_max, tc, tm):
    n, d = h.shape
    e = rank_col.shape[1]
    spec = pltpu.PrefetchScalarGridSpec(
        num_scalar_prefetch=6,
        grid=(ni_max,),
        in_specs=[pl.BlockSpec((tc, d), lambda k, t, c, *_: (c[k], 0)),
                  pl.BlockSpec((tc, e), lambda k, t, c, *_: (c[k], 0)),
                  pl.BlockSpec((tm, d), lambda k, t, c, *_: (t[k], 0))],
        out_specs=pl.BlockSpec((tc, d), lambda k, t, c, *_: (c[k], 0)),
    )
    return pl.pallas_call(
        _moe_combine_kernel,
        grid_spec=spec,
        out_shape=jax.ShapeDtypeStruct((n, d), F32),
        compiler_params=_params(("arbitrary",)),
        name="moe_combine",
    )(*items, h, rank_col, ys)


def _moe_grouped(a, wo, x, nw, rwt, wgu, wd, tc=MOE_CHUNK, tm=MOE_TILE, tr=MOE_PERM_TILE):
    gates, u, h = _router(a, wo, x, nw, rwt)
    rank_all = _moe_rank(gates, tc)
    tiles, by_tile, by_chunk, nt_max, ni_max = _moe_plan(rank_all, gates, tc, tm, tr)
    xs, gs = _moe_gather(u, rank_all, gates, by_tile, nt_max * tm, ni_max, tc, tr)
    ys = _moe_ffn(xs, gs, tiles, wgu, wd, nt_max, tm)
    rank_col = jnp.where(gates > 0.0, rank_all, -1.0).T
    return _moe_combine(h, rank_col, ys, by_chunk, ni_max, tc, tr)


def kernel(x, a_norm_w, a_w_in, a_q_norm_w, a_k_norm_w, a_w_out, kv_norm_w, kv_w, b_norm_w, b_w_q, b_w_out,
           ffn_norm_w, dense_w_gate_up, dense_w_down, moe_router_w, moe_w_gate_up, moe_w_down):
    b, s, d = x.shape
    n = b * s
    topk = min(MAX_TOPK, s // 4)
    tq_a, ch_a, tq_b = DSA_TQ, DSA_CH, SB_TQ
    h = x.reshape(n, d)

    w_in = jnp.pad(a_w_in[0], ((0, 0), (0, A_IN_PAD - A_IN))).astype(BF16)
    q, k, vt, qit, ki, wt = _dsa_prep(h, a_norm_w[0], w_in, a_q_norm_w[0], a_k_norm_w[0], b, s, tq_a, ch_a)
    o = _dsa_attn(qit, wt, ki, q, k, vt, b, s, tq_a, ch_a, topk)
    h = _ffn_dense(o, a_w_out[0].astype(BF16), h, ffn_norm_w[0],
                   dense_w_gate_up[0].astype(BF16), dense_w_down[0].astype(BF16))

    k_sb, vt_sb, q_sb = _sb_proj(h, kv_norm_w, b_norm_w[0], kv_w.astype(BF16),
                                 (b_w_q[0] * (HEAD_DIM ** -0.5 * LOG2E)).astype(BF16), b, s, tq_b)
    o = _sb_attn(q_sb, k_sb, vt_sb, b, s, tq_b)
    h = _moe_grouped(o, b_w_out[0].astype(BF16), h, ffn_norm_w[1], moe_router_w[0].T,
                     moe_w_gate_up[0].astype(BF16), moe_w_down[0].astype(BF16))
    return h.reshape(b, s, d)
```

```python
import functools

import jax
import jax.numpy as jnp
from jax import lax
from jax.experimental import pallas as pl
from jax.experimental.pallas import tpu as pltpu

D_MODEL = 1024
HEAD_DIM = 128
N_HEADS_A = 8
N_KV_HEADS_A = 2
GROUP_A = 4
N_IDX_HEADS = 8
IDX_DIM = 64
MAX_TOPK = 256
N_HEADS_B = 8
ROPE_THETA = 500000.0
ROT_FRAC = 4
NORM_EPS = 1e-6
D_FF = 2816
N_EXPERTS = 8

A_Q = N_HEADS_A * HEAD_DIM
A_K = N_KV_HEADS_A * HEAD_DIM
A_V = N_KV_HEADS_A * HEAD_DIM
A_QI = N_IDX_HEADS * IDX_DIM
A_IN = A_Q + A_K + A_V + A_QI + IDX_DIM + N_IDX_HEADS

LANE = 128
SUBLANES = 8
HALF_ROWS = 16
MXU_DIM = 256
VMEM_LIMIT = 56 * 1024 * 1024

A_IN_PAD = A_Q + A_K + A_V + A_QI + LANE
ROT_HALF = HEAD_DIM // ROT_FRAC // 2
ROT_HALF_IDX = IDX_DIM // ROT_FRAC // 2
VT_ROWS = HEAD_DIM + HALF_ROWS

DSA_TQ = MXU_DIM
DSA_CH = MXU_DIM
SB_TQ = MXU_DIM
SB_HEADS = 4
ROW_BLOCK = 512
FF_CHUNK = D_FF // 2
MOE_CHUNK = 512
MOE_TILE = 512
MOE_PERM_TILE = 256

INT_MIN = -(2 ** 31)
HALF_BITS = 16
HALF_SPAN = 2 ** HALF_BITS
HALF_MIN = -(2 ** 15)
F32_LOWEST = -3.4028234663852886e38
LOG2E = 1.4426950408889634
SB_DEAD_LOG2 = 151.0
FIX_CAP = 64

F32 = jnp.float32
BF16 = jnp.bfloat16
NT_DIMS = (((1,), (1,)), ((), ()))


def _params(sem):
    return pltpu.CompilerParams(dimension_semantics=sem, vmem_limit_bytes=VMEM_LIMIT)


def _rms(x, w):
    ms = jnp.mean(x * x, axis=-1, keepdims=True)
    return x * lax.rsqrt(ms + NORM_EPS) * w


def _sb_proj_kernel(x_ref, kvnw_ref, qnw_ref, wkv_ref, wq_ref, k_ref, vt_ref, q_ref):
    x = x_ref[...]
    r = x * lax.rsqrt(jnp.mean(x * x, axis=-1, keepdims=True) + NORM_EPS)
    d = k_ref.shape[1]
    kv = jnp.dot((r * kvnw_ref[...]).astype(BF16), wkv_ref[...], preferred_element_type=F32)
    k_ref[...] = kv[:, :d].astype(k_ref.dtype)
    vt_ref[0, 0] = kv[:, d:].T.astype(vt_ref.dtype)
    q_ref[...] = jnp.dot((r * qnw_ref[...]).astype(BF16), wq_ref[...],
                         preferred_element_type=F32).astype(q_ref.dtype)


def _sb_proj(x, kv_nw, q_nw, wkv, wq, b, s, ch):
    n, d = x.shape
    nb = s // ch
    row = pl.BlockSpec((ch, d), lambda i: (i, 0))
    vec = pl.BlockSpec((1, d), lambda i: (0, 0))
    return pl.pallas_call(
        _sb_proj_kernel,
        grid=(n // ch,),
        in_specs=[row, vec, vec,
                  pl.BlockSpec(wkv.shape, lambda i: (0, 0)),
                  pl.BlockSpec(wq.shape, lambda i: (0, 0))],
        out_specs=(row, pl.BlockSpec((1, 1, d, ch), lambda i: (i // nb, i % nb, 0, 0)), row),
        out_shape=(jax.ShapeDtypeStruct((n, d), BF16),
                   jax.ShapeDtypeStruct((b, nb, d, ch), BF16),
                   jax.ShapeDtypeStruct((n, d), BF16)),
        compiler_params=_params(("parallel",)),
        name="sb_proj",
    )(x, kv_nw.reshape(1, d), q_nw.reshape(1, d), wkv, wq)


def _rope(x, c, s_lo, s_hi, half):
    n = x.shape[-1]
    return x * c + pltpu.roll(x, n - half, 1) * s_lo + pltpu.roll(x, half, 1) * s_hi


def _dsa_prep_kernel(x_ref, nw_ref, w_ref, qw_ref, kw_ref, c_ref, slo_ref, shi_ref, ci_ref, sloi_ref, shii_ref,
                     q_ref, k_ref, vt_ref, qit_ref, ki_ref, wt_ref, p_ref, *, tq):
    bm = p_ref.shape[0]
    p_ref[...] = jnp.dot(_rms(x_ref[...], nw_ref[...]).astype(BF16), w_ref[...], preferred_element_type=F32)
    c, slo, shi = c_ref[...], slo_ref[...], shi_ref[...]
    ci, sloi, shii = ci_ref[...], sloi_ref[...], shii_ref[...]
    scale = HEAD_DIM ** -0.5 * LOG2E
    for h in range(N_HEADS_A):
        xh = _rms(p_ref[:, h * HEAD_DIM:(h + 1) * HEAD_DIM], qw_ref[...])
        q_ref[:, h * HEAD_DIM:(h + 1) * HEAD_DIM] = (_rope(xh, c, slo, shi, ROT_HALF) * scale).astype(BF16)
    for h in range(N_KV_HEADS_A):
        xh = _rms(p_ref[:, A_Q + h * HEAD_DIM:A_Q + (h + 1) * HEAD_DIM], kw_ref[...])
        k_ref[:, h * HEAD_DIM:(h + 1) * HEAD_DIM] = _rope(xh, c, slo, shi, ROT_HALF).astype(BF16)
    ones = jnp.ones((VT_ROWS - HEAD_DIM, bm), BF16)
    for h in range(N_KV_HEADS_A):
        v = p_ref[:, A_Q + A_K + h * HEAD_DIM:A_Q + A_K + (h + 1) * HEAD_DIM]
        vt_ref[0, 0, h * VT_ROWS:h * VT_ROWS + HEAD_DIM, :] = v.T.astype(BF16)
        vt_ref[0, 0, h * VT_ROWS + HEAD_DIM:(h + 1) * VT_ROWS, :] = ones
    off = A_Q + A_K + A_V
    for j in range(A_QI // LANE):
        xj = _rope(p_ref[:, off + j * LANE:off + (j + 1) * LANE], ci, sloi, shii, ROT_HALF_IDX)
        xt =xj.T.astype(BF16)
        for b in range(bm // tq):
            qit_ref[0, b, j * LANE:(j + 1) * LANE, :] = xt[:, b * tq:(b + 1) * tq]
    off += A_QI
    lane = lax.broadcasted_iota(jnp.int32, (1, LANE), 1)
    first = lane < IDX_DIM
    kw = p_ref[:, off:off + LANE]
    kw = _rope(kw, jnp.where(first, ci, 1.0), jnp.where(first, sloi, 0.0), jnp.where(first, shii, 0.0),
               ROT_HALF_IDX)
    ki_ref[...] = kw[:, :IDX_DIM].astype(BF16)
    wt = kw.T[IDX_DIM:IDX_DIM + N_IDX_HEADS, :] * (N_IDX_HEADS ** -0.5) * (IDX_DIM ** -0.5)
    for b in range(bm // tq):
        wt_ref[0, b] = wt[:, b * tq:(b + 1) * tq]


def _rope_tables(s, head, reps):
    r = head // ROT_FRAC
    half = r // 2
    inv = ROPE_THETA ** (-jnp.arange(half, dtype=F32) * 2.0 / r)
    ang = jnp.arange(s).astype(F32)[:, None] * inv[None, :]
    cos, sin = jnp.cos(ang), jnp.sin(ang)
    one = jnp.ones((s, head - r), F32)
    zero = jnp.zeros((s, head - r), F32)
    zh = jnp.zeros((s, half), F32)
    c = jnp.concatenate([cos, cos, one], axis=1)
    s_lo = jnp.concatenate([-sin, zh, zero], axis=1)
    s_hi = jnp.concatenate([zh, sin, zero], axis=1)
    return tuple(jnp.tile(t, (1, reps)) for t in (c, s_lo, s_hi))


def _dsa_prep(x, nw, w_in, qw, kw, b, s, tq, ch):
    n, d = x.shape
    bm = ch
    nb = s // bm
    c, slo, shi = _rope_tables(s, HEAD_DIM, 1)
    ci, sloi, shii = _rope_tables(s, IDX_DIM, 2)
    tab = pl.BlockSpec((bm, LANE), lambda i: (i % nb, 0))
    row = lambda w: pl.BlockSpec((bm, w), lambda i: (i, 0))
    out_shapes = (
        jax.ShapeDtypeStruct((n, A_Q), BF16),
        jax.ShapeDtypeStruct((n, A_K), BF16),
        jax.ShapeDtypeStruct((b, s // ch, N_KV_HEADS_A * VT_ROWS, ch), BF16),
        jax.ShapeDtypeStruct((b, s // tq, A_QI, tq), BF16),
        jax.ShapeDtypeStruct((n, IDX_DIM), BF16),
        jax.ShapeDtypeStruct((b, s // tq, N_IDX_HEADS, tq), F32),
    )
    out_specs = (
        row(A_Q), row(A_K),
        pl.BlockSpec((1, 1, N_KV_HEADS_A * VT_ROWS, ch), lambda i: (i // nb, i % nb, 0, 0)),
        pl.BlockSpec((1, bm // tq, A_QI, tq), lambda i: (i // nb, i % nb, 0, 0)),
        row(IDX_DIM),
        pl.BlockSpec((1, bm // tq, N_IDX_HEADS, tq), lambda i: (i // nb, i % nb, 0, 0)),
    )
    return pl.pallas_call(
        functools.partial(_dsa_prep_kernel, tq=tq),
        grid=(n // bm,),
        in_specs=[row(d),
                  pl.BlockSpec((1, d), lambda i: (0, 0)),
                  pl.BlockSpec(w_in.shape, lambda i: (0, 0)),
                  pl.BlockSpec((1, HEAD_DIM), lambda i: (0, 0)),
                  pl.BlockSpec((1, HEAD_DIM), lambda i: (0, 0)),
                  tab, tab, tab, tab, tab, tab],
        out_specs=out_specs,
        out_shape=out_shapes,
        scratch_shapes=[pltpu.VMEM((bm, A_IN_PAD), F32)],
        compiler_params=_params(("parallel",)),
        name="dsa_prep",
    )(x, nw.reshape(1, d), w_in, qw.reshape(1, -1), kw.reshape(1, -1), c, slo, shi, ci, sloi, shii)


def _dsa_attn_kernel(qit_ref, wt_ref, ki_ref, q_ref, k_ref, vt_ref, o_ref,
                     sc_ref, hi_ref, lo_ref, thrf_ref, cnt_ref, acc_ref, *, tq, ch, topk):
    i = pl.program_id(1)
    nch = ((i + 1) * tq + ch - 1) // ch
    t_row = i * tq + lax.broadcasted_iota(jnp.int32, (1, tq), 1)

    qit = jnp.concatenate([qit_ref[0, 0, h * IDX_DIM:(h + 1) * IDX_DIM, :] for h in range(N_IDX_HEADS)],
                          axis=1)
    wt = wt_ref[0, 0]

    def to_key(x):
        bits = lax.bitcast_convert_type(x, jnp.int32)
        return bits ^ ((bits >> 31) & 0x7FFFFFFF)

    npair = (nch + 1) // 2

    def score_pair(c2, diagonal):
        starts = [pl.multiple_of(c2 * (2 * ch) + half * ch, ch) for half in range(2)]
        rels = [jnp.dot(ki_ref[pl.ds(s0, ch), :], qit, preferred_element_type=F32) for s0 in starts]
        for s0, rel in zip(starts, rels):
            sc = jnp.zeros((ch, tq), F32)
            for h in range(N_IDX_HEADS):
                sc = sc + jnp.maximum(rel[:, h * tq:(h + 1) * tq], 0.0) * wt[h:h + 1, :]
            sc = jnp.where(sc == 0.0, 0.0, sc)
            key = to_key(sc)
            if diagonal:
                causal = (s0 + lax.broadcasted_iota(jnp.int32, (ch, tq), 0)) <= t_row
                sc = jnp.where(causal, sc, -jnp.inf)
                key = jnp.where(causal, key, INT_MIN)
            sc_ref[pl.ds(s0, ch), :] = sc
            hi_ref[pl.ds(s0, ch), :] = (key >> HALF_BITS).astype(jnp.int16)

    def score_body(c2, carry):
        score_pair(c2, False)
        return carry

    lax.fori_loop(0, npair - 1, score_body, 0)
    score_pair(npair - 1, True)

    def count16(ref, thr):
        t16 = jnp.broadcast_to(thr.astype(jnp.int16), (HALF_ROWS, tq))

        def body(c2, acc):
            s0 = pl.multiple_of(c2 * (2 * ch), 2 * ch)
            x = ref[pl.ds(s0, 2 * ch), :].reshape(2 * ch // HALF_ROWS, HALF_ROWS, tq)
            hit = jnp.where(x >= t16[None], jnp.int16(1), jnp.int16(0))
            parts = [hit[r] for r in range(2 * ch // HALF_ROWS)]
            while len(parts) > 1:
                parts = [a + b for a, b in zip(parts[0::2], parts[1::2])]
            return acc + parts[0]

        acc = lax.fori_loop(0, npair, body, jnp.zeros((HALF_ROWS, tq), jnp.int16))
        return jnp.sum(acc.astype(jnp.int32), axis=0, keepdims=True)

    def bisect_pass(ref, want, bit, st):
        thr, cnt, over = st
        cand = thr + (jnp.int32(1) << bit)
        c = count16(ref, cand)
        take = c >= want
        return jnp.where(take, cand, thr), jnp.where(take, c, cnt), jnp.where(take, over, c)

    def bisect_start(ref, want, cnt_floor):
        c0 = count16(ref, jnp.zeros((1, tq), jnp.int32))
        pos = c0 >= want
        return jnp.where(pos, 0, HALF_MIN), jnp.where(pos, c0, cnt_floor), jnp.where(pos, 0, c0)

    short = t_row < topk
    live = jnp.logical_not(short)
    want_hi = jnp.full((1, tq), topk, jnp.int32)
    st = bisect_start(hi_ref, want_hi, jnp.full((1, tq), npair * 2 * ch, jnp.int32))
    thr_hi, cnt_hi, above = lax.fori_loop(0, 15, lambda it, s: bisect_pass(hi_ref, want_hi, 14 - it, s), st)
    want_lo = topk - above

    def lo_body(c2, carry):
        s0 = pl.multiple_of(c2 * (2 * ch), 2 * ch)
        key = to_key(sc_ref[pl.ds(s0, 2 * ch), :])
        lo = (key & (HALF_SPAN - 1)) + HALF_MIN
        lo_ref[pl.ds(s0, 2 * ch), :] = jnp.where((key >> HALF_BITS) == thr_hi, lo, HALF_MIN).astype(jnp.int16)
        return carry

    lax.fori_loop(0, npair, lo_body, 0)

    def n_open(cnt):
        return jnp.sum(jnp.where(jnp.logical_and(live, cnt != want_lo), 1, 0))

    def lo_cond(s):
        return jnp.logical_and(s[0] < 15, s[2] > 0)

    def lo_step(s):
        it, st, _ = s
        for k in range(3):
            st = bisect_pass(lo_ref, want_lo, 14 - it - k, st)
        return it + 3, st, n_open(st[1])

    st = bisect_start(lo_ref, want_lo, cnt_hi - above)
    _, (thr_lo, _, _), _ = lax.while_loop(lo_cond, lo_step, (jnp.int32(0), st, n_open(st[1])))
    thr_key = thr_hi * HALF_SPAN + (thr_lo - HALF_MIN)
    thr0 = lax.bitcast_convert_type(thr_key ^ ((thr_key >> 31) & 0x7FFFFFFF), F32)

    def sweep(pick, init, fold, join):
        def body(c2, acc):
            s0 = pl.multiple_of(c2 * (2 * ch), 2 * ch)
            v = pick(sc_ref[pl.ds(s0, 2 * ch), :]).reshape(2 * ch // SUBLANES, SUBLANES, tq)
            return join(acc, fold(v, axis=0))
        acc = lax.fori_loop(0, npair, body, jnp.full((SUBLANES, tq), init))
        return fold(acc, axis=0, keepdims=True)

    def count_where(cond):
        return sweep(lambda x: jnp.where(cond(x), 1, 0), jnp.int32(0), jnp.sum, jnp.add)

    def lowest_where(cond):
        return sweep(lambda x: jnp.where(cond(x), x, jnp.inf), jnp.float32(jnp.inf), jnp.min, jnp.minimum)

    thr0 = jnp.where(short, F32_LOWEST, thr0)
    thrf_ref[...] = thr0
    cnt_ref[...] = count_where(lambda x: x >= thr0)

    def n_rows(cond):
        return jnp.sum(jnp.where(jnp.logical_and(live, cond(cnt_ref[...])), 1, 0))

    def fix_cond(s):
        return jnp.logical_and(s[0] < FIX_CAP, s[1] > 0)

    @pl.when(n_rows(lambda c: c != topk) > 0)
    def _():
        def grow(s):
            thr, cnt = thrf_ref[...], cnt_ref[...]
            nxt = sweep(lambda x: jnp.where(x < thr, x, -jnp.inf), jnp.float32(-jnp.inf), jnp.max, jnp.maximum)
            move = jnp.logical_and(jnp.logical_and(live, cnt < topk), nxt > -jnp.inf)
            thr = jnp.where(move, nxt, thr)
            thrf_ref[...] = thr
            cnt_ref[...] = jnp.where(move, count_where(lambda x: x >= thr), cnt)
            return s[0] + 1, n_rows(lambda c: c < topk)

        lax.while_loop(fix_cond, grow, (jnp.int32(0), n_rows(lambda c: c < topk)))

        def shrink(s):
            thr, cnt = thrf_ref[...], cnt_ref[...]
            many = jnp.logical_and(live, cnt > topk)
            cur = lowest_where(lambda x: x >= thr)
            same = count_where(lambda x: x == cur)
            drop = jnp.logical_and(many, cnt - same >= topk)
            nxt = lowest_where(lambda x: x > cur)
            thrf_ref[...] = jnp.where(drop, nxt, jnp.where(many, cur, thr))
            cnt_ref[...] = jnp.where(drop, cnt - same, cnt)
            return s[0] + 1, jnp.sum(jnp.where(drop, 1, 0))

        lax.while_loop(fix_cond, shrink, (jnp.int32(0), n_rows(lambda c: c > topk)))

    @pl.when(n_rows(lambda c: c > topk) > 0)
    def _():
        tied = jnp.logical_and(live, cnt_ref[...] > topk)
        thr = jnp.where(tied, thrf_ref[...], jnp.inf)
        need = (topk - count_where(lambda x: x > thr)).astype(F32)
        r = lax.broadcasted_iota(jnp.int32, (ch, ch), 0)
        cc = lax.broadcasted_iota(jnp.int32, (ch, ch), 1)
        tri = jnp.where(cc <= r, 1.0, 0.0).astype(BF16)

        def tie_body(c, run):
            s0 = pl.multiple_of(c * ch, ch)
            x = sc_ref[pl.ds(s0, ch), :]
            eq = x == thr
            eqf = jnp.where(eq, 1.0, 0.0)
            pre = jnp.dot(tri, eqf.astype(BF16), preferred_element_type=F32) + run
            sc_ref[pl.ds(s0, ch), :] = jnp.where(jnp.logical_and(eq, pre > need), -jnp.inf, x)
            return run + jnp.sum(eqf, axis=0, keepdims=True)

        lax.fori_loop(0, nch, tie_body, jnp.zeros((1, tq), F32))

    thr = thrf_ref[...]

    heads = range(N_HEADS_A)
    acc_ref[...] = jnp.zeros(acc_ref.shape, F32)

    def att_body(c2, carry):
        s0 = pl.multiple_of(c2 * (2 * ch), 2 * ch)
        bias = jnp.where(sc_ref[pl.ds(s0, 2 * ch), :] >= thr, 0.0, -jnp.inf)
        lgs = [lax.dot_general(k_ref[pl.ds(s0, 2 * ch), (h // GROUP_A) * HEAD_DIM:(h // GROUP_A + 1) * HEAD_DIM],
                               q_ref[:, h * HEAD_DIM:(h + 1) * HEAD_DIM], NT_DIMS,
                               preferred_element_type=F32) for h in heads]
        lgs = [(lg + bias).astype(BF16) for lg in lgs]
        m_new = [jnp.maximum(carry[h], jnp.max(lgs[h], axis=0, keepdims=True).astype(F32)) for h in heads]
        m_safe = [jnp.where(m == -jnp.inf, 0.0, m) for m in m_new]
        ps = [jnp.exp2(lgs[h] - m_safe[h].astype(BF16)) for h in heads]
        pvs = [sum(jnp.dot(vt_ref[0, 2 * c2 + half, (h // GROUP_A) * VT_ROWS:(h // GROUP_A + 1) * VT_ROWS, :],
                           ps[h][half * ch:(half + 1) * ch], preferred_element_type=F32) for half in range(2))
               for h in heads]
        for h in heads:
            acc_ref[h] = acc_ref[h] * jnp.exp2(carry[h] - m_safe[h]) + pvs[h]
        return tuple(m_new)

    m0 = jnp.full((1, tq), -jnp.inf, F32)
    lax.fori_loop(0, npair, att_body, (m0,) * N_HEADS_A)
    for h in heads:
        og = acc_ref[h, :HEAD_DIM, :] / acc_ref[h, HEAD_DIM:HEAD_DIM + 1, :]
        o_ref[:, h * HEAD_DIM:(h + 1) * HEAD_DIM] = og.T.astype(o_ref.dtype)


def _dsa_attn(qit, wt, ki, q, k, vt, b, s, tq, ch, topk):
    assert tq == ch and s % (2 * ch) == 0, "the chunk-pair loops assume square tiles and an even chunk count"
    nq = s // tq
    return pl.pallas_call(
        functools.partial(_dsa_attn_kernel, tq=tq, ch=ch, topk=topk),
        grid=(b, nq),
        in_specs=[pl.BlockSpec((1, 1, A_QI, tq), lambda bb, i: (bb, i, 0, 0)),
                  pl.BlockSpec((1, 1, N_IDX_HEADS, tq), lambda bb, i: (bb, i, 0, 0)),
                  pl.BlockSpec((s, IDX_DIM), lambda bb, i: (bb, 0)),
                  pl.BlockSpec((tq, A_Q), lambda bb, i: (bb * nq + i, 0)),
                  pl.BlockSpec((s, A_K), lambda bb, i: (bb, 0)),
                  pl.BlockSpec((1, s // ch, N_KV_HEADS_A * VT_ROWS, ch), lambda bb, i: (bb, 0, 0, 0))],
        out_specs=pl.BlockSpec((tq, A_Q), lambda bb, i: (bb * nq + i, 0)),
        out_shape=jax.ShapeDtypeStruct((b * s, A_Q), BF16),
        scratch_shapes=[pltpu.VMEM((s, tq), F32),
                        pltpu.VMEM((s, tq), jnp.int16),
                        pltpu.VMEM((s, tq), jnp.int16),
                        pltpu.VMEM((1, tq), F32),
                        pltpu.VMEM((1, tq), jnp.int32),
                        pltpu.VMEM((N_HEADS_A, VT_ROWS, tq), F32)],
        compiler_params=_params(("parallel", "arbitrary")),
        name="dsa_attn",
    )(qit, wt, ki, q, k, vt)


def _sb_attn_kernel(q_ref, k_ref, vt_ref, o_ref, acc_ref, *, tq, hb):
    i = pl.program_id(2)
    ch = tq
    t_row = i * tq + lax.broadcasted_iota(jnp.int32, (1, tq), 1)
    r = lax.broadcasted_iota(jnp.int32, (ch, ch), 0)
    cc = lax.broadcasted_iota(jnp.int32, (ch, ch), 1)
    tri = jnp.where(cc >= r, 1.0, 0.0).astype(BF16)
    acc_ref[...] = jnp.zeros(acc_ref.shape, F32)

    def cond(st):
        j, _, low = st
        return jnp.logical_and(j <= i, low <= SB_DEAD_LOG2)

    def body(st):
        j, runs, _ = st
        c = i - j
        s0 = pl.multiple_of(c * ch, ch)
        strict = (s0 + lax.broadcasted_iota(jnp.int32, (ch, tq), 0)) < t_row
        heads = [slice(h * HEAD_DIM, (h + 1) * HEAD_DIM) for h in range(hb)]
        zs = [lax.dot_general(k_ref[pl.ds(s0, ch), hs], q_ref[:, hs], NT_DIMS, preferred_element_type=F32)
              for hs in heads]
        sps = [jnp.where(strict, jnp.maximum(z, 0.0) + jnp.log(1.0 + jnp.exp2(-jnp.abs(z))) * LOG2E, 0.0)
               for z in zs]
        his = [sp.astype(BF16) for sp in sps]
        los = [(sp - hi.astype(F32)).astype(BF16) for sp, hi in zip(sps, his)]
        cums = [jnp.dot(tri, hi, preferred_element_type=F32) + jnp.dot(tri, lo, preferred_element_type=F32) + run
                for hi, lo, run in zip(his, los, runs)]
        aas = [jnp.where(strict, jnp.exp2(z - cum), 0.0).astype(BF16) for z, cum in zip(zs, cums)]
        pvs = [jnp.dot(vt_ref[0, c, hs, :], a, preferred_element_type=F32) for hs, a in zip(heads, aas)]
        new_runs = [run + jnp.sum(sp, axis=0, keepdims=True) for run, sp in zip(runs, sps)]
        for h in range(hb):
            acc_ref[h] += pvs[h]
        low = jnp.min(functools.reduce(jnp.minimum, new_runs))
        return j + 1, tuple(new_runs), low

    run0 = tuple(jnp.zeros((1, tq), F32) for _ in range(hb))
    lax.while_loop(cond, body, (jnp.int32(0), run0, jnp.float32(0.0)))
    for h in range(hb):
        o_ref[:, h * HEAD_DIM:(h + 1) * HEAD_DIM] = acc_ref[h].T.astype(o_ref.dtype)


def _sb_attn(q, k, vt, b, s, tq, hb=SB_HEADS):
    nq = s // tq
    d = N_HEADS_B * HEAD_DIM
    w = hb * HEAD_DIM
    return pl.pallas_call(
        functools.partial(_sb_attn_kernel, tq=tq, hb=hb),
        grid=(b, N_HEADS_B // hb, nq),
        in_specs=[pl.BlockSpec((tq, w), lambda bb, h, i: (bb * nq + i, h)),
                  pl.BlockSpec((s, w), lambda bb, h, i: (bb, h)),
                  pl.BlockSpec((1, s // tq, w, tq), lambda bb, h, i: (bb, 0, h, 0))],
        out_specs=pl.BlockSpec((tq, w), lambda bb, h, i: (bb * nq + i, h)),
        out_shape=jax.ShapeDtypeStruct((b * s, d), BF16),
        scratch_shapes=[pltpu.VMEM((hb, HEAD_DIM, tq), F32)],
        compiler_params=_params(("parallel", "parallel", "arbitrary")),
        name="sb_attn",
    )(q, k, vt)


def _silu_mul(g, up):
    return g * (1.0 / (1.0 + jnp.exp(-g))) * up


def _ffn_kernel(a_ref, wo_ref, x_ref, nw_ref, wg_ref, wu_ref, wd_ref, o_ref, u_ref):
    j = pl.program_id(1)

    @pl.when(j == 0)
    def _():
        x = x_ref[...] + jnp.dot(a_ref[...], wo_ref[...], preferred_element_type=F32)
        u_ref[...] = _rms(x, nw_ref[...]).astype(BF16)
        o_ref[...] = x

    u = u_ref[...]
    g = jnp.dot(u, wg_ref[...], preferred_element_type=F32)
    up = jnp.dot(u, wu_ref[...], preferred_element_type=F32)
    o_ref[...] += jnp.dot(_silu_mul(g, up).astype(BF16), wd_ref[...], preferred_element_type=F32)


def _ffn_dense(a, wo, x, nw, wgu, wd, bm=ROW_BLOCK, fc=FF_CHUNK):
    n, d = x.shape
    nj = D_FF // fc
    return pl.pallas_call(
        _ffn_kernel,
        grid=(n // bm, nj),
        in_specs=[pl.BlockSpec((bm, a.shape[1]), lambda i, j: (i, 0)),
                  pl.BlockSpec(wo.shape, lambda i, j: (0, 0)),
                  pl.BlockSpec((bm, d), lambda i, j: (i, 0)),
                  pl.BlockSpec((1, d), lambda i, j: (0, 0)),
                  pl.BlockSpec((d, fc), lambda i, j: (0, j)),
                  pl.BlockSpec((d, fc), lambda i, j: (0, j + nj)),
                  pl.BlockSpec((fc, d), lambda i, j: (j, 0))],
        out_specs=pl.BlockSpec((bm, d), lambda i, j: (i, 0)),
        out_shape=jax.ShapeDtypeStruct((n, d), F32),
        scratch_shapes=[pltpu.VMEM((bm, d), BF16)],
        compiler_params=_params(("parallel", "arbitrary")),
        name="ffn_dense",
    )(a, wo, x, nw.reshape(1, d), wgu, wgu, wd)


def _router_kernel(a_ref, wo_ref, x_ref, nw_ref, rwt_ref, g_ref, u_ref, h_ref):
    h = x_ref[...] + jnp.dot(a_ref[...], wo_ref[...], preferred_element_type=F32)
    h_ref[...] = h
    u = _rms(h, nw_ref[...])
    u_ref[...] = u.astype(BF16)
    lt = lax.dot_general(rwt_ref[...], u, NT_DIMS, preferred_element_type=F32,
                         precision=lax.Precision.HIGHEST)
    e = lax.broadcasted_iota(jnp.int32, lt.shape, 0)
    m1 = jnp.max(lt, axis=0, keepdims=True)
    i1 = jnp.min(jnp.where(lt == m1, e, N_EXPERTS), axis=0, keepdims=True)
    rest = jnp.where(e == i1, -jnp.inf, lt)
    m2 = jnp.max(rest, axis=0, keepdims=True)
    i2 = jnp.min(jnp.where(rest == m2, e, N_EXPERTS), axis=0, keepdims=True)
    w2 = jnp.exp(m2 - m1)
    den = 1.0 + w2
    g_ref[...] = jnp.where(e == i1, 1.0 / den, 0.0) + jnp.where(e == i2, w2 / den, 0.0)


def _router(a, wo, x, nw, rwt, bm=ROW_BLOCK):
    n, d = x.shape
    return pl.pallas_call(
        _router_kernel,
        grid=(n // bm,),
        in_specs=[pl.BlockSpec((bm, a.shape[1]), lambda i: (i, 0)),
                  pl.BlockSpec(wo.shape, lambda i: (0, 0)),
                  pl.BlockSpec((bm, d), lambda i: (i, 0)),
                  pl.BlockSpec((1, d), lambda i: (0, 0)),
                  pl.BlockSpec((N_EXPERTS, d), lambda i: (0, 0))],
        out_specs=(pl.BlockSpec((N_EXPERTS, bm), lambda i: (0, i)),
                   pl.BlockSpec((bm, d), lambda i: (i, 0)),
                   pl.BlockSpec((bm, d), lambda i: (i, 0))),
        out_shape=(jax.ShapeDtypeStruct((N_EXPERTS, n), F32),
                   jax.ShapeDtypeStruct((n, d), BF16),
                   jax.ShapeDtypeStruct((n, d), F32)),
        compiler_params=_params(("parallel",)),
        name="moe_router",
    )(a, wo, x, nw.reshape(1, d), rwt)


def _moe_rank_kernel(g_ref, rank_ref, carry_ref):
    @pl.when(pl.program_id(0) == 0)
    def _():
        carry_ref[...] = jnp.zeros(carry_ref.shape, F32)

    tc = g_ref.shape[1]
    m = jnp.where(g_ref[...] > 0.0, 1.0, 0.0)
    r = lax.broadcasted_iota(jnp.int32, (tc, tc), 0)
    c = lax.broadcasted_iota(jnp.int32, (tc, tc), 1)
    before = jnp.where(r < c, 1.0, 0.0).astype(BF16)
    rank_ref[...] = jnp.dot(m.astype(BF16), before, preferred_element_type=F32) + carry_ref[:, :1]
    carry_ref[...] = carry_ref[...] + jnp.sum(m, axis=1, keepdims=True)


def _moe_rank(gates, tc):
    e, n = gates.shape
    return pl.pallas_call(
        _moe_rank_kernel,
        grid=(n // tc,),
        in_specs=[pl.BlockSpec((e, tc), lambda i: (0, i))],
        out_specs=pl.BlockSpec((e, tc), lambda i: (0, i)),
        out_shape=jax.ShapeDtypeStruct((e, n), F32),
        scratch_shapes=[pltpu.VMEM((e, LANE), F32)],
        compiler_params=_params(("arbitrary",)),
        name="moe_rank",
    )(gates)


def _moe_plan(rank_all, gates, tc, tm, tr):
    e_n, n = rank_all.shape
    nc = n // tc
    nt_max = 2 * n // tm + e_n
    nr_max = nt_max * (tm // tr)
    ni_max = e_n * nc + nr_max
    slots = tc // tr + 1
    i32 = jnp.int32
    cstart = rank_all[:, ::tc].astype(i32)
    counts = (rank_all[:, -1] + jnp.where(gates[:, -1] > 0.0, 1.0, 0.0)).astype(i32)
    cend = jnp.concatenate([cstart[:, 1:], counts[:, None]], axis=1)
    tiles = (counts + tm - 1) // tm
    tend = jnp.cumsum(tiles)
    tstart = tend - tiles
    total = tend[-1]
    t_idx = jnp.arange(nt_max, dtype=i32)
    tvalid = t_idx < total
    texp = jnp.sum((t_idx[:, None] >= tend[None, :]).astype(i32), axis=1)
    last_exp = jnp.sum((total - 1 >= tend).astype(i32))
    texp = jnp.where(tvalid, texp, last_exp)
    tsrc = jnp.where(tvalid, t_idx, 0)

    nonempty = cend > cstart
    t0 = cstart // tr
    t1 = (cend - 1) // tr
    local = jnp.stack([t0 + j for j in range(slots)], axis=-1)
    ok = jnp.stack([nonempty & (t0 + j <= t1) for j in range(slots)], axis=-1).reshape(-1)
    shape = local.shape
    tile = (tstart[:, None, None] * (tm // tr) + local).reshape(-1)
    base = (local * tr).reshape(-1)
    exp = jnp.broadcast_to(jnp.arange(e_n, dtype=i32)[:, None, None], shape).reshape(-1)
    chunk = jnp.broadcast_to(jnp.arange(nc, dtype=i32)[None, :, None], shape).reshape(-1)
    n_items = jnp.sum(ok.astype(i32))
    k_idx = jnp.arange(ni_max, dtype=i32)
    valid = k_idx < n_items

    def ordered(key, group):
        order = jnp.argsort(jnp.where(ok, key, jnp.int32(2 ** 30)))[:ni_max]
        src = jnp.where(valid, order, order[n_items - 1])
        grp = group[src]
        first = jnp.concatenate([jnp.ones((1,), bool), grp[1:] != grp[:-1]]) & valid
        return (tile[src], chunk[src], exp[src], base[src], first.astype(i32), valid.astype(i32))

    by_tile = ordered(tile * nc + chunk, tile)
    by_chunk = ordered(chunk * nr_max + tile, chunk)
    return (texp, tvalid.astype(i32), tsrc), by_tile, by_chunk, nt_max, ni_max


def _moe_gather_kernel(tile_s, chunk_s, exp_s, base_s, first_s, valid_s, u_ref, rank_ref, g_ref,
                       x_zero_ref, gs_zero_ref, x_ref, gs_ref):
    del x_zero_ref, gs_zero_ref
    k = pl.program_id(0)
    tm, tc = x_ref.shape[0], u_ref.shape[0]

    @pl.when(valid_s[k] == 1)
    def _():
        e = exp_s[k]
        gate = g_ref[pl.ds(e, 1), :]
        rank = jnp.where(gate > 0.0, rank_ref[pl.ds(e, 1), :], -1.0)
        rows = (base_s[k] + lax.broadcasted_iota(jnp.int32, (tm, tc), 0)).astype(F32)
        p = jnp.where(rows == rank, 1.0, 0.0)
        xs = jnp.dot(p.astype(BF16), u_ref[...], preferred_element_type=F32).astype(BF16)
        gsel = jnp.sum(p * gate, axis=1, keepdims=True)

        @pl.when(first_s[k] == 1)
        def _():
            x_ref[...] = xs
            gs_ref[...] = gsel

        @pl.when(first_s[k] == 0)
        def _():
            x_ref[...] = x_ref[...] + xs
            gs_ref[...] = gs_ref[...] + gsel


def _moe_gather(u, rank_all, gates, items, n_rows, ni_max, tc, tr):
    n, d = u.shape
    e = gates.shape[0]
    n_prefetch = len(items)
    spec = pltpu.PrefetchScalarGridSpec(
        num_scalar_prefetch=n_prefetch,
        grid=(ni_max,),
        in_specs=[pl.BlockSpec((tc, d), lambda k, t, c, *_: (c[k], 0)),
                  pl.BlockSpec((e, tc), lambda k, t, c, *_: (0, c[k])),
                  pl.BlockSpec((e, tc), lambda k, t, c, *_: (0, c[k])),
                  pl.BlockSpec(memory_space=pl.ANY),
                  pl.BlockSpec(memory_space=pl.ANY)],
        out_specs=(pl.BlockSpec((tr, d), lambda k, t, c, *_: (t[k], 0)),
                   pl.BlockSpec((tr, 1), lambda k, t, c, *_: (t[k], 0))),
    )
    return pl.pallas_call(
        _moe_gather_kernel,
        grid_spec=spec,
        out_shape=(jax.ShapeDtypeStruct((n_rows, d), BF16),
                   jax.ShapeDtypeStruct((n_rows, 1), F32)),
        input_output_aliases={n_prefetch + 3: 0, n_prefetch + 4: 1},
        compiler_params=_params(("arbitrary",)),
        name="moe_gather",
    )(*items, u, rank_all, gates, jnp.zeros((n_rows, d), BF16), jnp.zeros((n_rows, 1), F32))


def _moe_ffn_kernel(texp_s, tvalid_s, tsrc_s, x_ref, gs_ref, wg_ref, wu_ref, wd_ref, o_ref, acc_ref):
    t = pl.program_id(0)
    j = pl.program_id(1)
    valid = tvalid_s[t] == 1

    @pl.when(valid)
    def _():
        x = x_ref[...]
        g = jnp.dot(x, wg_ref[0], preferred_element_type=F32)
        up = jnp.dot(x, wu_ref[0], preferred_element_type=F32)
        hm = (_silu_mul(g, up) * gs_ref[...]).astype(BF16)
        y = jnp.dot(hm, wd_ref[0], preferred_element_type=F32)

        @pl.when(j == 0)
        def _():
            acc_ref[...] = y

        @pl.when(j > 0)
        def _():
            acc_ref[...] += y

    @pl.when(j == pl.num_programs(1) - 1)
    def _():
        @pl.when(valid)
        def _():
            o_ref[...] = acc_ref[...].astype(o_ref.dtype)

        @pl.when(jnp.logical_not(valid))
        def _():
            o_ref[...] = jnp.zeros(o_ref.shape, o_ref.dtype)


def _moe_ffn(xs, gs, tiles, wgu, wd, nt_max, tm, fc=FF_CHUNK):
    d = xs.shape[1]
    nj = D_FF // fc
    spec = pltpu.PrefetchScalarGridSpec(
        num_scalar_prefetch=3,
        grid=(nt_max, nj),
        in_specs=[pl.BlockSpec((tm, d), lambda t, j, te, tv, ts: (ts[t], 0)),
                  pl.BlockSpec((tm, 1), lambda t, j, te, tv, ts: (ts[t], 0)),
                  pl.BlockSpec((1, d, fc), lambda t, j, te, tv, ts: (te[t], 0, j)),
                  pl.BlockSpec((1, d, fc), lambda t, j, te, tv, ts: (te[t], 0, j + nj)),
                  pl.BlockSpec((1, fc, d), lambda t, j, te, tv, ts: (te[t], j, 0))],
        out_specs=pl.BlockSpec((tm, d), lambda t, j, te, tv, ts: (t, 0)),
        scratch_shapes=[pltpu.VMEM((tm, d), F32)],
    )
    return pl.pallas_call(
        _moe_ffn_kernel,
        grid_spec=spec,
        out_shape=jax.ShapeDtypeStruct((nt_max * tm, d), BF16),
        compiler_params=_params(("arbitrary", "arbitrary")),
        name="moe_ffn",
    )(*tiles, xs, gs, wgu, wgu, wd)


def _moe_combine_kernel(tile_s, chunk_s, exp_s, base_s, first_s, valid_s, h_ref, rcol_ref, ys_ref, o_ref):
    k = pl.program_id(0)
    tc, tm = h_ref.shape[0], ys_ref.shape[0]

    @pl.when(first_s[k] == 1)
    def _():
        o_ref[...] = h_ref[...]

    @pl.when(valid_s[k] == 1)
    def _():
        lane = lax.broadcasted_iota(jnp.int32, rcol_ref.shape, 1)
        rank = jnp.sum(jnp.where(lane == exp_s[k], rcol_ref[...], 0.0), axis=1, keepdims=True)
        cols = (base_s[k] + lax.broadcasted_iota(jnp.int32, (tc, tm), 1)).astype(F32)
        p = jnp.where(cols == rank, 1.0, 0.0).astype(BF16)
        o_ref[...] += jnp.dot(p, ys_ref[...], preferred_element_type=F32)


def _moe_combine(h, rank_col, ys, items, ni_max, tc, tm):
    n, d = h.shape
    e = rank_col.shape[1]
    spec = pltpu.PrefetchScalarGridSpec(
        num_scalar_prefetch=6,
        grid=(ni_max,),
        in_specs=[pl.BlockSpec((tc, d), lambda k, t, c, *_: (c[k], 0)),
                  pl.BlockSpec((tc, e), lambda k, t, c, *_: (c[k], 0)),
                  pl.BlockSpec((tm, d), lambda k, t, c, *_: (t[k], 0))],
        out_specs=pl.BlockSpec((tc, d), lambda k, t, c, *_: (c[k], 0)),
    )
    return pl.pallas_call(
        _moe_combine_kernel,
        grid_spec=spec,
        out_shape=jax.ShapeDtypeStruct((n, d), F32),
        compiler_params=_params(("arbitrary",)),
        name="moe_combine",
    )(*items, h, rank_col, ys)


def _moe_grouped(a, wo, x, nw, rwt, wgu, wd, tc=MOE_CHUNK, tm=MOE_TILE, tr=MOE_PERM_TILE):
    gates, u, h = _router(a, wo, x, nw, rwt)
    rank_all = _moe_rank(gates, tc)
    tiles, by_tile, by_chunk, nt_max, ni_max = _moe_plan(rank_all, gates, tc, tm, tr)
    xs, gs = _moe_gather(u, rank_all, gates, by_tile, nt_max * tm, ni_max, tc, tr)
    ys = _moe_ffn(xs, gs, tiles, wgu, wd, nt_max, tm)
    rank_col = jnp.where(gates > 0.0, rank_all, -1.0).T
    return _moe_combine(h, rank_col, ys, by_chunk, ni_max, tc, tr)


def kernel(x, a_norm_w, a_w_in, a_q_norm_w, a_k_norm_w, a_w_out, kv_norm_w, kv_w, b_norm_w, b_w_q, b_w_out,
           ffn_norm_w, dense_w_gate_up, dense_w_down, moe_router_w, moe_w_gate_up, moe_w_down):
    b, s, d = x.shape
    n = b * s
    topk = min(MAX_TOPK, s // 4)
    tq_a, ch_a, tq_b = DSA_TQ, DSA_CH, SB_TQ
    h = x.reshape(n, d)

    w_in = jnp.pad(a_w_in[0], ((0, 0), (0, A_IN_PAD - A_IN))).astype(BF16)
    q, k, vt, qit, ki, wt = _dsa_prep(h, a_norm_w[0], w_in, a_q_norm_w[0], a_k_norm_w[0], b, s, tq_a, ch_a)
    o = _dsa_attn(qit, wt, ki, q, k, vt, b, s, tq_a, ch_a, topk)
    h = _ffn_dense(o, a_w_out[0].astype(BF16), h, ffn_norm_w[0],
                   dense_w_gate_up[0].astype(BF16), dense_w_down[0].astype(BF16))

    k_sb, vt_sb, q_sb = _sb_proj(h, kv_norm_w, b_norm_w[0], kv_w.astype(BF16),
                                 (b_w_q[0] * (HEAD_DIM ** -0.5 * LOG2E)).astype(BF16), b, s, tq_b)
    o = _sb_attn(q_sb, k_sb, vt_sb, b, s, tq_b)
    h = _moe_grouped(o, b_w_out[0].astype(BF16), h, ffn_norm_w[1], moe_router_w[0].T,
                     moe_w_gate_up[0].astype(BF16), moe_w_down[0].astype(BF16))
    return h.reshape(b, s, d)
```

```python
import functools

import jax
import jax.numpy as jnp
from jax import lax
from jax.experimental import pallas as pl
from jax.experimental.pallas import tpu as pltpu

D_MODEL = 1024
HEAD_DIM = 128
N_HEADS_A = 8
N_KV_HEADS_A = 2
GROUP_A = 4
N_IDX_HEADS = 8
IDX_DIM = 64
MAX_TOPK = 256
N_HEADS_B = 8
ROPE_THETA = 500000.0
ROT_FRAC = 4
NORM_EPS = 1e-6
D_FF = 2816
N_EXPERTS = 8

A_Q = N_HEADS_A * HEAD_DIM
A_K = N_KV_HEADS_A * HEAD_DIM
A_V = N_KV_HEADS_A * HEAD_DIM
A_QI = N_IDX_HEADS * IDX_DIM
A_IN = A_Q + A_K + A_V + A_QI + IDX_DIM + N_IDX_HEADS

LANE = 128
SUBLANES = 8
HALF_ROWS = 16
MXU_DIM = 256
VMEM_LIMIT = 56 * 1024 * 1024

A_IN_PAD = A_Q + A_K + A_V + A_QI + LANE
ROT_HALF = HEAD_DIM // ROT_FRAC // 2
ROT_HALF_IDX = IDX_DIM // ROT_FRAC // 2
VT_ROWS = HEAD_DIM + HALF_ROWS

DSA_TQ = MXU_DIM
DSA_CH = MXU_DIM
SB_TQ = MXU_DIM
SB_HEADS = 4
ROW_BLOCK = 512
FF_CHUNK = D_FF // 2
MOE_CHUNK = 512
MOE_TILE = 512
MOE_PERM_TILE = 256

INT_MIN = -(2 ** 31)
HALF_BITS = 16
HALF_SPAN = 2 ** HALF_BITS
HALF_MIN = -(2 ** 15)
F32_LOWEST = -3.4028234663852886e38
LOG2E = 1.4426950408889634
SB_DEAD_LOG2 = 151.0
FIX_CAP = 64

F32 = jnp.float32
BF16 = jnp.bfloat16
NT_DIMS = (((1,), (1,)), ((), ()))


def _params(sem):
    return pltpu.CompilerParams(dimension_semantics=sem, vmem_limit_bytes=VMEM_LIMIT)


def _rms(x, w):
    ms = jnp.mean(x * x, axis=-1, keepdims=True)
    return x * lax.rsqrt(ms + NORM_EPS) * w


def _sb_proj_kernel(x_ref, kvnw_ref, qnw_ref, wkv_ref, wq_ref, k_ref, vt_ref, q_ref):
    x = x_ref[...]
    r = x * lax.rsqrt(jnp.mean(x * x, axis=-1, keepdims=True) + NORM_EPS)
    d = k_ref.shape[1]
    kv = jnp.dot((r * kvnw_ref[...]).astype(BF16), wkv_ref[...], preferred_element_type=F32)
    k_ref[...] = kv[:, :d].astype(k_ref.dtype)
    vt_ref[0, 0] = kv[:, d:].T.astype(vt_ref.dtype)
    q_ref[...] = jnp.dot((r * qnw_ref[...]).astype(BF16), wq_ref[...],
                         preferred_element_type=F32).astype(q_ref.dtype)


def _sb_proj(x, kv_nw, q_nw, wkv, wq, b, s, ch):
    n, d = x.shape
    nb = s // ch
    row = pl.BlockSpec((ch, d), lambda i: (i, 0))
    vec = pl.BlockSpec((1, d), lambda i: (0, 0))
    return pl.pallas_call(
        _sb_proj_kernel,
        grid=(n // ch,),
        in_specs=[row, vec, vec,
                  pl.BlockSpec(wkv.shape, lambda i: (0, 0)),
                  pl.BlockSpec(wq.shape, lambda i: (0, 0))],
        out_specs=(row, pl.BlockSpec((1, 1, d, ch), lambda i: (i // nb, i % nb, 0, 0)), row),
        out_shape=(jax.ShapeDtypeStruct((n, d), BF16),
                   jax.ShapeDtypeStruct((b, nb, d, ch), BF16),
                   jax.ShapeDtypeStruct((n, d), BF16)),
        compiler_params=_params(("parallel",)),
        name="sb_proj",
    )(x, kv_nw.reshape(1, d), q_nw.reshape(1, d), wkv, wq)


def _rope(x, c, s_lo, s_hi, half):
    n = x.shape[-1]
    return x * c + pltpu.roll(x, n - half, 1) * s_lo + pltpu.roll(x, half, 1) * s_hi


def _dsa_prep_kernel(x_ref, nw_ref, w_ref, qw_ref, kw_ref, c_ref, slo_ref, shi_ref, ci_ref, sloi_ref, shii_ref,
                     q_ref, k_ref, vt_ref, qit_ref, ki_ref, wt_ref, p_ref, *, tq):
    bm = p_ref.shape[0]
    p_ref[...] = jnp.dot(_rms(x_ref[...], nw_ref[...]).astype(BF16), w_ref[...], preferred_element_type=F32)
    c, slo, shi = c_ref[...], slo_ref[...], shi_ref[...]
    ci, sloi, shii = ci_ref[...], sloi_ref[...], shii_ref[...]
    scale = HEAD_DIM ** -0.5 * LOG2E
    for h in range(N_HEADS_A):
        xh = _rms(p_ref[:, h * HEAD_DIM:(h + 1) * HEAD_DIM], qw_ref[...])
        q_ref[:, h * HEAD_DIM:(h + 1) * HEAD_DIM] = (_rope(xh, c, slo, shi, ROT_HALF) * scale).astype(BF16)
    for h in range(N_KV_HEADS_A):
        xh = _rms(p_ref[:, A_Q + h * HEAD_DIM:A_Q + (h + 1) * HEAD_DIM], kw_ref[...])
        k_ref[:, h * HEAD_DIM:(h + 1) * HEAD_DIM] = _rope(xh, c, slo, shi, ROT_HALF).astype(BF16)
    ones = jnp.ones((VT_ROWS - HEAD_DIM, bm), BF16)
    for h in range(N_KV_HEADS_A):
        v = p_ref[:, A_Q + A_K + h * HEAD_DIM:A_Q + A_K + (h + 1) * HEAD_DIM]
        vt_ref[0, 0, h * VT_ROWS:h * VT_ROWS + HEAD_DIM, :] = v.T.astype(BF16)
        vt_ref[0, 0, h * VT_ROWS + HEAD_DIM:(h + 1) * VT_ROWS, :] = ones
    off = A_Q + A_K + A_V
    for j in range(A_QI // LANE):
        xj = _rope(p_ref[:, off + j * LANE:off + (j + 1) * LANE], ci, sloi, shii, ROT_HALF_IDX)
        xt =xj.T.astype(BF16)
        for b in range(bm // tq):
            qit_ref[0, b, j * LANE:(j + 1) * LANE, :] = xt[:, b * tq:(b + 1) * tq]
    off += A_QI
    lane = lax.broadcasted_iota(jnp.int32, (1, LANE), 1)
    first = lane < IDX_DIM
    kw = p_ref[:, off:off + LANE]
    kw = _rope(kw, jnp.where(first, ci, 1.0), jnp.where(first, sloi, 0.0), jnp.where(first, shii, 0.0),
               ROT_HALF_IDX)
    ki_ref[...] = kw[:, :IDX_DIM].astype(BF16)
    wt = kw.T[IDX_DIM:IDX_DIM + N_IDX_HEADS, :] * (N_IDX_HEADS ** -0.5) * (IDX_DIM ** -0.5)
    for b in range(bm // tq):
        wt_ref[0, b] = wt[:, b * tq:(b + 1) * tq]


def _rope_tables(s, head, reps):
    r = head // ROT_FRAC
    half = r // 2
    inv = ROPE_THETA ** (-jnp.arange(half, dtype=F32) * 2.0 / r)
    ang = jnp.arange(s).astype(F32)[:, None] * inv[None, :]
    cos, sin = jnp.cos(ang), jnp.sin(ang)
    one = jnp.ones((s, head - r), F32)
    zero = jnp.zeros((s, head - r), F32)
    zh = jnp.zeros((s, half), F32)
    c = jnp.concatenate([cos, cos, one], axis=1)
    s_lo = jnp.concatenate([-sin, zh, zero], axis=1)
    s_hi = jnp.concatenate([zh, sin, zero], axis=1)
    return tuple(jnp.tile(t, (1, reps)) for t in (c, s_lo, s_hi))


def _dsa_prep(x, nw, w_in, qw, kw, b, s, tq, ch):
    n, d = x.shape
    bm = ch
    nb = s // bm
    c, slo, shi = _rope_tables(s, HEAD_DIM, 1)
    ci, sloi, shii = _rope_tables(s, IDX_DIM, 2)
    tab = pl.BlockSpec((bm, LANE), lambda i: (i % nb, 0))
    row = lambda w: pl.BlockSpec((bm, w), lambda i: (i, 0))
    out_shapes = (
        jax.ShapeDtypeStruct((n, A_Q), BF16),
        jax.ShapeDtypeStruct((n, A_K), BF16),
        jax.ShapeDtypeStruct((b, s // ch, N_KV_HEADS_A * VT_ROWS, ch), BF16),
        jax.ShapeDtypeStruct((b, s // tq, A_QI, tq), BF16),
        jax.ShapeDtypeStruct((n, IDX_DIM), BF16),
        jax.ShapeDtypeStruct((b, s // tq, N_IDX_HEADS, tq), F32),
    )
    out_specs = (
        row(A_Q), row(A_K),
        pl.BlockSpec((1, 1, N_KV_HEADS_A * VT_ROWS, ch), lambda i: (i // nb, i % nb, 0, 0)),
        pl.BlockSpec((1, bm // tq, A_QI, tq), lambda i: (i // nb, i % nb, 0, 0)),
        row(IDX_DIM),
        pl.BlockSpec((1, bm // tq, N_IDX_HEADS, tq), lambda i: (i // nb, i % nb, 0, 0)),
    )
    return pl.pallas_call(
        functools.partial(_dsa_prep_kernel, tq=tq),
        grid=(n // bm,),
        in_specs=[row(d),
                  pl.BlockSpec((1, d), lambda i: (0, 0)),
                  pl.BlockSpec(w_in.shape, lambda i: (0, 0)),
                  pl.BlockSpec((1, HEAD_DIM), lambda i: (0, 0)),
                  pl.BlockSpec((1, HEAD_DIM), lambda i: (0, 0)),
                  tab, tab, tab, tab, tab, tab],
        out_specs=out_specs,
        out_shape=out_shapes,
        scratch_shapes=[pltpu.VMEM((bm, A_IN_PAD), F32)],
        compiler_params=_params(("parallel",)),
        name="dsa_prep",
    )(x, nw.reshape(1, d), w_in, qw.reshape(1, -1), kw.reshape(1, -1), c, slo, shi, ci, sloi, shii)


def _dsa_attn_kernel(qit_ref, wt_ref, ki_ref, q_ref, k_ref, vt_ref, o_ref,
                     sc_ref, hi_ref, lo_ref, thrf_ref, cnt_ref, acc_ref, *, tq, ch, topk):
    i = pl.program_id(1)
    nch = ((i + 1) * tq + ch - 1) // ch
    t_row = i * tq + lax.broadcasted_iota(jnp.int32, (1, tq), 1)

    qit = jnp.concatenate([qit_ref[0, 0, h * IDX_DIM:(h + 1) * IDX_DIM, :] for h in range(N_IDX_HEADS)],
                          axis=1)
    wt = wt_ref[0, 0]

    def to_key(x):
        bits = lax.bitcast_convert_type(x, jnp.int32)
        return bits ^ ((bits >> 31) & 0x7FFFFFFF)

    npair = (nch + 1) // 2

    def score_pair(c2, diagonal):
        starts = [pl.multiple_of(c2 * (2 * ch) + half * ch, ch) for half in range(2)]
        rels = [jnp.dot(ki_ref[pl.ds(s0, ch), :], qit, preferred_element_type=F32) for s0 in starts]
        for s0, rel in zip(starts, rels):
            sc = jnp.zeros((ch, tq), F32)
            for h in range(N_IDX_HEADS):
                sc = sc + jnp.maximum(rel[:, h * tq:(h + 1) * tq], 0.0) * wt[h:h + 1, :]
            sc = jnp.where(sc == 0.0, 0.0, sc)
            key = to_key(sc)
            if diagonal:
                causal = (s0 + lax.broadcasted_iota(jnp.int32, (ch, tq), 0)) <= t_row
                sc = jnp.where(causal, sc, -jnp.inf)
                key = jnp.where(causal, key, INT_MIN)
            sc_ref[pl.ds(s0, ch), :] = sc
            hi_ref[pl.ds(s0, ch), :] = (key >> HALF_BITS).astype(jnp.int16)

    def score_body(c2, carry):
        score_pair(c2, False)
        return carry

    lax.fori_loop(0, npair - 1, score_body, 0)
    score_pair(npair - 1, True)

    def count16(ref, thr):
        t16 = jnp.broadcast_to(thr.astype(jnp.int16), (HALF_ROWS, tq))

        def body(c2, acc):
            s0 = pl.multiple_of(c2 * (2 * ch), 2 * ch)
            x = ref[pl.ds(s0, 2 * ch), :].reshape(2 * ch // HALF_ROWS, HALF_ROWS, tq)
            hit = jnp.where(x >= t16[None], jnp.int16(1), jnp.int16(0))
            parts = [hit[r] for r in range(2 * ch // HALF_ROWS)]
            while len(parts) > 1:
                parts = [a + b for a, b in zip(parts[0::2], parts[1::2])]
            return acc + parts[0]

        acc = lax.fori_loop(0, npair, body, jnp.zeros((HALF_ROWS, tq), jnp.int16))
        return jnp.sum(acc.astype(jnp.int32), axis=0, keepdims=True)

    def bisect_pass(ref, want, bit, st):
        thr, cnt, over = st
        cand = thr + (jnp.int32(1) << bit)
        c = count16(ref, cand)
        take = c >= want
        return jnp.where(take, cand, thr), jnp.where(take, c, cnt), jnp.where(take, over, c)

    def bisect_start(ref, want, cnt_floor):
        c0 = count16(ref, jnp.zeros((1, tq), jnp.int32))
        pos = c0 >= want
        return jnp.where(pos, 0, HALF_MIN), jnp.where(pos, c0, cnt_floor), jnp.where(pos, 0, c0)

    short = t_row < topk
    live = jnp.logical_not(short)
    want_hi = jnp.full((1, tq), topk, jnp.int32)
    st = bisect_start(hi_ref, want_hi, jnp.full((1, tq), npair * 2 * ch, jnp.int32))
    thr_hi, cnt_hi, above = lax.fori_loop(0, 15, lambda it, s: bisect_pass(hi_ref, want_hi, 14 - it, s), st)
    want_lo = topk - above

    def lo_body(c2, carry):
        s0 = pl.multiple_of(c2 * (2 * ch), 2 * ch)
        key = to_key(sc_ref[pl.ds(s0, 2 * ch), :])
        lo = (key & (HALF_SPAN - 1)) + HALF_MIN
        lo_ref[pl.ds(s0, 2 * ch), :] = jnp.where((key >> HALF_BITS) == thr_hi, lo, HALF_MIN).astype(jnp.int16)
        return carry

    lax.fori_loop(0, npair, lo_body, 0)

    def n_open(cnt):
        return jnp.sum(jnp.where(jnp.logical_and(live, cnt != want_lo), 1, 0))

    def lo_cond(s):
        return jnp.logical_and(s[0] < 15, s[2] > 0)

    def lo_step(s):
        it, st, _ = s
        for k in range(3):
            st = bisect_pass(lo_ref, want_lo, 14 - it - k, st)
        return it + 3, st, n_open(st[1])

    st = bisect_start(lo_ref, want_lo, cnt_hi - above)
    _, (thr_lo, _, _), _ = lax.while_loop(lo_cond, lo_step, (jnp.int32(0), st, n_open(st[1])))
    thr_key = thr_hi * HALF_SPAN + (thr_lo - HALF_MIN)
    thr0 = lax.bitcast_convert_type(thr_key ^ ((thr_key >> 31) & 0x7FFFFFFF), F32)

    def sweep(pick, init, fold, join):
        def body(c2, acc):
            s0 = pl.multiple_of(c2 * (2 * ch), 2 * ch)
            v = pick(sc_ref[pl.ds(s0, 2 * ch), :]).reshape(2 * ch // SUBLANES, SUBLANES, tq)
            return join(acc, fold(v, axis=0))
        acc = lax.fori_loop(0, npair, body, jnp.full((SUBLANES, tq), init))
        return fold(acc, axis=0, keepdims=True)

    def count_where(cond):
        return sweep(lambda x: jnp.where(cond(x), 1, 0), jnp.int32(0), jnp.sum, jnp.add)

    def lowest_where(cond):
        return sweep(lambda x: jnp.where(cond(x), x, jnp.inf), jnp.float32(jnp.inf), jnp.min, jnp.minimum)

    thr0 = jnp.where(short, F32_LOWEST, thr0)
    thrf_ref[...] = thr0
    cnt_ref[...] = count_where(lambda x: x >= thr0)

    def n_rows(cond):
        return jnp.sum(jnp.where(jnp.logical_and(live, cond(cnt_ref[...])), 1, 0))

    def fix_cond(s):
        return jnp.logical_and(s[0] < FIX_CAP, s[1] > 0)

    @pl.when(n_rows(lambda c: c != topk) > 0)
    def _():
        def grow(s):
            thr, cnt = thrf_ref[...], cnt_ref[...]
            nxt = sweep(lambda x: jnp.where(x < thr, x, -jnp.inf), jnp.float32(-jnp.inf), jnp.max, jnp.maximum)
            move = jnp.logical_and(jnp.logical_and(live, cnt < topk), nxt > -jnp.inf)
            thr = jnp.where(move, nxt, thr)
            thrf_ref[...] = thr
            cnt_ref[...] = jnp.where(move, count_where(lambda x: x >= thr), cnt)
            return s[0] + 1, n_rows(lambda c: c < topk)

        lax.while_loop(fix_cond, grow, (jnp.int32(0), n_rows(lambda c: c < topk)))

        def shrink(s):
            thr, cnt = thrf_ref[...], cnt_ref[...]
            many = jnp.logical_and(live, cnt > topk)
            cur = lowest_where(lambda x: x >= thr)
            same = count_where(lambda x: x == cur)
            drop = jnp.logical_and(many, cnt - same >= topk)
            nxt = lowest_where(lambda x: x > cur)
            thrf_ref[...] = jnp.where(drop, nxt, jnp.where(many, cur, thr))
            cnt_ref[...] = jnp.where(drop, cnt - same, cnt)
            return s[0] + 1, jnp.sum(jnp.where(drop, 1, 0))

        lax.while_loop(fix_cond, shrink, (jnp.int32(0), n_rows(lambda c: c > topk)))

    @pl.when(n_rows(lambda c: c > topk) > 0)
    def _():
        tied = jnp.logical_and(live, cnt_ref[...] > topk)
        thr = jnp.where(tied, thrf_ref[...], jnp.inf)
        need = (topk - count_where(lambda x: x > thr)).astype(F32)
        r = lax.broadcasted_iota(jnp.int32, (ch, ch), 0)
        cc = lax.broadcasted_iota(jnp.int32, (ch, ch), 1)
        tri = jnp.where(cc <= r, 1.0, 0.0).astype(BF16)

        def tie_body(c, run):
            s0 = pl.multiple_of(c * ch, ch)
            x = sc_ref[pl.ds(s0, ch), :]
            eq = x == thr
            eqf = jnp.where(eq, 1.0, 0.0)
            pre = jnp.dot(tri, eqf.astype(BF16), preferred_element_type=F32) + run
            sc_ref[pl.ds(s0, ch), :] = jnp.where(jnp.logical_and(eq, pre > need), -jnp.inf, x)
            return run + jnp.sum(eqf, axis=0, keepdims=True)

        lax.fori_loop(0, nch, tie_body, jnp.zeros((1, tq), F32))

    thr = thrf_ref[...]

    heads = range(N_HEADS_A)
    acc_ref[...] = jnp.zeros(acc_ref.shape, F32)

    def att_body(c2, carry):
        s0 = pl.multiple_of(c2 * (2 * ch), 2 * ch)
        bias = jnp.where(sc_ref[pl.ds(s0, 2 * ch), :] >= thr, 0.0, -jnp.inf)
        lgs = [lax.dot_general(k_ref[pl.ds(s0, 2 * ch), (h // GROUP_A) * HEAD_DIM:(h // GROUP_A + 1) * HEAD_DIM],
                               q_ref[:, h * HEAD_DIM:(h + 1) * HEAD_DIM], NT_DIMS,
                               preferred_element_type=F32) for h in heads]
        lgs = [(lg + bias).astype(BF16) for lg in lgs]
        m_new = [jnp.maximum(carry[h], jnp.max(lgs[h], axis=0, keepdims=True).astype(F32)) for h in heads]
        m_safe = [jnp.where(m == -jnp.inf, 0.0, m) for m in m_new]
        ps = [jnp.exp2(lgs[h] - m_safe[h].astype(BF16)) for h in heads]
        pvs = [sum(jnp.dot(vt_ref[0, 2 * c2 + half, (h // GROUP_A) * VT_ROWS:(h // GROUP_A + 1) * VT_ROWS, :],
                           ps[h][half * ch:(half + 1) * ch], preferred_element_type=F32) for half in range(2))
               for h in heads]
        for h in heads:
            acc_ref[h] = acc_ref[h] * jnp.exp2(carry[h] - m_safe[h]) + pvs[h]
        return tuple(m_new)

    m0 = jnp.full((1, tq), -jnp.inf, F32)
    lax.fori_loop(0, npair, att_body, (m0,) * N_HEADS_A)
    for h in heads:
        og = acc_ref[h, :HEAD_DIM, :] / acc_ref[h, HEAD_DIM:HEAD_DIM + 1, :]
        o_ref[:, h * HEAD_DIM:(h + 1) * HEAD_DIM] = og.T.astype(o_ref.dtype)


def _dsa_attn(qit, wt, ki, q, k, vt, b, s, tq, ch, topk):
    assert tq == ch and s % (2 * ch) == 0, "the chunk-pair loops assume square tiles and an even chunk count"
    nq = s // tq
    return pl.pallas_call(
        functools.partial(_dsa_attn_kernel, tq=tq, ch=ch, topk=topk),
        grid=(b, nq),
        in_specs=[pl.BlockSpec((1, 1, A_QI, tq), lambda bb, i: (bb, i, 0, 0)),
                  pl.BlockSpec((1, 1, N_IDX_HEADS, tq), lambda bb, i: (bb, i, 0, 0)),
                  pl.BlockSpec((s, IDX_DIM), lambda bb, i: (bb, 0)),
                  pl.BlockSpec((tq, A_Q), lambda bb, i: (bb * nq + i, 0)),
                  pl.BlockSpec((s, A_K), lambda bb, i: (bb, 0)),
                  pl.BlockSpec((1, s // ch, N_KV_HEADS_A * VT_ROWS, ch), lambda bb, i: (bb, 0, 0, 0))],
        out_specs=pl.BlockSpec((tq, A_Q), lambda bb, i: (bb * nq + i, 0)),
        out_shape=jax.ShapeDtypeStruct((b * s, A_Q), BF16),
        scratch_shapes=[pltpu.VMEM((s, tq), F32),
                        pltpu.VMEM((s, tq), jnp.int16),
                        pltpu.VMEM((s, tq), jnp.int16),
                        pltpu.VMEM((1, tq), F32),
                        pltpu.VMEM((1, tq), jnp.int32),
                        pltpu.VMEM((N_HEADS_A, VT_ROWS, tq), F32)],
        compiler_params=_params(("parallel", "arbitrary")),
        name="dsa_attn",
    )(qit, wt, ki, q, k, vt)


def _sb_attn_kernel(q_ref, k_ref, vt_ref, o_ref, acc_ref, *, tq, hb):
    i = pl.program_id(2)
    ch = tq
    t_row = i * tq + lax.broadcasted_iota(jnp.int32, (1, tq), 1)
    r = lax.broadcasted_iota(jnp.int32, (ch, ch), 0)
    cc = lax.broadcasted_iota(jnp.int32, (ch, ch), 1)
    tri = jnp.where(cc >= r, 1.0, 0.0).astype(BF16)
    acc_ref[...] = jnp.zeros(acc_ref.shape, F32)

    def cond(st):
        j, _, low = st
        return jnp.logical_and(j <= i, low <= SB_DEAD_LOG2)

    def step(st, diagonal):
        j, runs, _ = st
        c = i - j
        s0 = pl.multiple_of(c * ch, ch)
        strict = (s0 + lax.broadcasted_iota(jnp.int32, (ch, tq), 0)) < t_row if diagonal else None
        mask = (lambda v: jnp.where(strict, v, 0.0)) if diagonal else (lambda v: v)
        heads = [slice(h * HEAD_DIM, (h + 1) * HEAD_DIM) for h in range(hb)]
        zs = [lax.dot_general(k_ref[pl.ds(s0, ch), hs], q_ref[:, hs], NT_DIMS, preferred_element_type=F32)
              for hs in heads]
        sps = [mask(jnp.maximum(z, 0.0) + jnp.log(1.0 + jnp.exp2(-jnp.abs(z))) * LOG2E)
               for z in zs]
        his = [sp.astype(BF16) for sp in sps]
        los = [(sp - hi.astype(F32)).astype(BF16) for sp, hi in zip(sps, his)]
        cums = [jnp.dot(tri, hi, preferred_element_type=F32) + jnp.dot(tri, lo, preferred_element_type=F32) + run
                for hi, lo, run in zip(his, los, runs)]
        aas = [mask(jnp.exp2(z - cum)).astype(BF16) for z, cum in zip(zs, cums)]
        pvs = [jnp.dot(vt_ref[0, c, hs, :], a, preferred_element_type=F32) for hs, a in zip(heads, aas)]
        new_runs = [run + jnp.sum(sp, axis=0, keepdims=True) for run, sp in zip(runs, sps)]
        for h in range(hb):
            acc_ref[h] += pvs[h]
        low = jnp.min(functools.reduce(jnp.minimum, new_runs))
        return j + 1, tuple(new_runs), low

    run0 = tuple(jnp.zeros((1, tq), F32) for _ in range(hb))
    first = step((jnp.int32(0), run0, jnp.float32(0.0)), True)
    lax.while_loop(cond, lambda st: step(st, False), first)
    for h in range(hb):
        o_ref[:, h * HEAD_DIM:(h + 1) * HEAD_DIM] = acc_ref[h].T.astype(o_ref.dtype)


def _sb_attn(q, k, vt, b, s, tq, hb=SB_HEADS):
    nq = s // tq
    d = N_HEADS_B * HEAD_DIM
    w = hb * HEAD_DIM
    return pl.pallas_call(
        functools.partial(_sb_attn_kernel, tq=tq, hb=hb),
        grid=(b, N_HEADS_B // hb, nq),
        in_specs=[pl.BlockSpec((tq, w), lambda bb, h, i: (bb * nq + i, h)),
                  pl.BlockSpec((s, w), lambda bb, h, i: (bb, h)),
                  pl.BlockSpec((1, s // tq, w, tq), lambda bb, h, i: (bb, 0, h, 0))],
        out_specs=pl.BlockSpec((tq, w), lambda bb, h, i: (bb * nq + i, h)),
        out_shape=jax.ShapeDtypeStruct((b * s, d), BF16),
        scratch_shapes=[pltpu.VMEM((hb, HEAD_DIM, tq), F32)],
        compiler_params=_params(("parallel", "parallel", "arbitrary")),
        name="sb_attn",
    )(q, k, vt)


def _silu_mul(g, up):
    return g * (1.0 / (1.0 + jnp.exp(-g))) * up


def _ffn_kernel(a_ref, wo_ref, x_ref, nw_ref, wg_ref, wu_ref, wd_ref, o_ref, u_ref):
    j = pl.program_id(1)

    @pl.when(j == 0)
    def _():
        x = x_ref[...] + jnp.dot(a_ref[...], wo_ref[...], preferred_element_type=F32)
        u_ref[...] = _rms(x, nw_ref[...]).astype(BF16)
        o_ref[...] = x

    u = u_ref[...]
    g = jnp.dot(u, wg_ref[...], preferred_element_type=F32)
    up = jnp.dot(u, wu_ref[...], preferred_element_type=F32)
    o_ref[...] += jnp.dot(_silu_mul(g, up).astype(BF16), wd_ref[...], preferred_element_type=F32)


def _ffn_dense(a, wo, x, nw, wgu, wd, bm=2 * ROW_BLOCK, fc=FF_CHUNK):
    n, d = x.shape
    nj = D_FF // fc
    return pl.pallas_call(
        _ffn_kernel,
        grid=(n // bm, nj),
        in_specs=[pl.BlockSpec((bm, a.shape[1]), lambda i, j: (i, 0)),
                  pl.BlockSpec(wo.shape, lambda i, j: (0, 0)),
                  pl.BlockSpec((bm, d), lambda i, j: (i, 0)),
                  pl.BlockSpec((1, d), lambda i, j: (0, 0)),
                  pl.BlockSpec((d, fc), lambda i, j: (0, j)),
                  pl.BlockSpec((d, fc), lambda i, j: (0, j + nj)),
                  pl.BlockSpec((fc, d), lambda i, j: (j, 0))],
        out_specs=pl.BlockSpec((bm, d), lambda i, j: (i, 0)),
        out_shape=jax.ShapeDtypeStruct((n, d), F32),
        scratch_shapes=[pltpu.VMEM((bm, d), BF16)],
        compiler_params=_params(("parallel", "arbitrary")),
        name="ffn_dense",
    )(a, wo, x, nw.reshape(1, d), wgu, wgu, wd)


def _router_kernel(a_ref, wo_ref, x_ref, nw_ref, rwt_ref, g_ref, u_ref, h_ref):
    h = x_ref[...] + jnp.dot(a_ref[...], wo_ref[...], preferred_element_type=F32)
    h_ref[...] = h
    u = _rms(h, nw_ref[...])
    u_ref[...] = u.astype(BF16)
    lt = lax.dot_general(rwt_ref[...], u, NT_DIMS, preferred_element_type=F32,
                         precision=lax.Precision.HIGHEST)
    e = lax.broadcasted_iota(jnp.int32, lt.shape, 0)
    m1 = jnp.max(lt, axis=0, keepdims=True)
    i1 = jnp.min(jnp.where(lt == m1, e, N_EXPERTS), axis=0, keepdims=True)
    rest = jnp.where(e == i1, -jnp.inf, lt)
    m2 = jnp.max(rest, axis=0, keepdims=True)
    i2 = jnp.min(jnp.where(rest == m2, e, N_EXPERTS), axis=0, keepdims=True)
    w2 = jnp.exp(m2 - m1)
    den = 1.0 + w2
    g_ref[...] = jnp.where(e == i1, 1.0 / den, 0.0) + jnp.where(e == i2, w2 / den, 0.0)


def _router(a, wo, x, nw, rwt, bm=ROW_BLOCK):
    n, d = x.shape
    return pl.pallas_call(
        _router_kernel,
        grid=(n // bm,),
        in_specs=[pl.BlockSpec((bm, a.shape[1]), lambda i: (i, 0)),
                  pl.BlockSpec(wo.shape, lambda i: (0, 0)),
                  pl.BlockSpec((bm, d), lambda i: (i, 0)),
                  pl.BlockSpec((1, d), lambda i: (0, 0)),
                  pl.BlockSpec((N_EXPERTS, d), lambda i: (0, 0))],
        out_specs=(pl.BlockSpec((N_EXPERTS, bm), lambda i: (0, i)),
                   pl.BlockSpec((bm, d), lambda i: (i, 0)),
                   pl.BlockSpec((bm, d), lambda i: (i, 0))),
        out_shape=(jax.ShapeDtypeStruct((N_EXPERTS, n), F32),
                   jax.ShapeDtypeStruct((n, d), BF16),
                   jax.ShapeDtypeStruct((n, d), F32)),
        compiler_params=_params(("parallel",)),
        name="moe_router",
    )(a, wo, x, nw.reshape(1, d), rwt)


def _moe_rank_kernel(g_ref, rank_ref, carry_ref):
    @pl.when(pl.program_id(0) == 0)
    def _():
        carry_ref[...] = jnp.zeros(carry_ref.shape, F32)

    tc = g_ref.shape[1]
    m = jnp.where(g_ref[...] > 0.0, 1.0, 0.0)
    r = lax.broadcasted_iota(jnp.int32, (tc, tc), 0)
    c = lax.broadcasted_iota(jnp.int32, (tc, tc), 1)
    before = jnp.where(r < c, 1.0, 0.0).astype(BF16)
    rank_ref[...] = jnp.dot(m.astype(BF16), before, preferred_element_type=F32) + carry_ref[:, :1]
    carry_ref[...] = carry_ref[...] + jnp.sum(m, axis=1, keepdims=True)


def _moe_rank(gates, tc):
    e, n = gates.shape
    return pl.pallas_call(
        _moe_rank_kernel,
        grid=(n // tc,),
        in_specs=[pl.BlockSpec((e, tc), lambda i: (0, i))],
        out_specs=pl.BlockSpec((e, tc), lambda i: (0, i)),
        out_shape=jax.ShapeDtypeStruct((e, n), F32),
        scratch_shapes=[pltpu.VMEM((e, LANE), F32)],
        compiler_params=_params(("arbitrary",)),
        name="moe_rank",
    )(gates)


def _moe_plan(rank_all, gates, tc, tm, tr):
    e_n, n = rank_all.shape
    nc = n // tc
    nt_max = 2 * n // tm + e_n
    nr_max = nt_max * (tm // tr)
    ni_max = e_n * nc + nr_max
    slots = tc // tr + 1
    i32 = jnp.int32
    cstart = rank_all[:, ::tc].astype(i32)
    counts = (rank_all[:, -1] + jnp.where(gates[:, -1] > 0.0, 1.0, 0.0)).astype(i32)
    cend = jnp.concatenate([cstart[:, 1:], counts[:, None]], axis=1)
    tiles = (counts + tm - 1) // tm
    tend = jnp.cumsum(tiles)
    tstart = tend - tiles
    total = tend[-1]
    t_idx = jnp.arange(nt_max, dtype=i32)
    tvalid = t_idx < total
    texp = jnp.sum((t_idx[:, None] >= tend[None, :]).astype(i32), axis=1)
    last_exp = jnp.sum((total - 1 >= tend).astype(i32))
    texp = jnp.where(tvalid, texp, last_exp)
    tsrc = jnp.where(tvalid, t_idx, 0)

    nonempty = cend > cstart
    t0 = cstart // tr
    t1 = (cend - 1) // tr
    local = jnp.stack([t0 + j for j in range(slots)], axis=-1)
    ok = jnp.stack([nonempty & (t0 + j <= t1) for j in range(slots)], axis=-1).reshape(-1)
    shape = local.shape
    tile = (tstart[:, None, None] * (tm // tr) + local).reshape(-1)
    base = (local * tr).reshape(-1)
    exp = jnp.broadcast_to(jnp.arange(e_n, dtype=i32)[:, None, None], shape).reshape(-1)
    chunk = jnp.broadcast_to(jnp.arange(nc, dtype=i32)[None, :, None], shape).reshape(-1)
    n_items = jnp.sum(ok.astype(i32))
    k_idx = jnp.arange(ni_max, dtype=i32)
    valid = k_idx < n_items

    def ordered(key, group):
        order = jnp.argsort(jnp.where(ok, key, jnp.int32(2 ** 30)))[:ni_max]
        src = jnp.where(valid, order, order[n_items - 1])
        grp = group[src]
        first = jnp.concatenate([jnp.ones((1,), bool), grp[1:] != grp[:-1]]) & valid
        return (tile[src], chunk[src], exp[src], base[src], first.astype(i32), valid.astype(i32))

    by_tile = ordered(tile * nc + chunk, tile)
    by_chunk = ordered(chunk * nr_max + tile, chunk)
    return (texp, tvalid.astype(i32), tsrc), by_tile, by_chunk, nt_max, ni_max


def _moe_gather_kernel(tile_s, chunk_s, exp_s, base_s, first_s, valid_s, u_ref, rank_ref, g_ref,
                       x_zero_ref, gs_zero_ref, x_ref, gs_ref):
    del x_zero_ref, gs_zero_ref
    k = pl.program_id(0)
    tm, tc = x_ref.shape[0], u_ref.shape[0]

    @pl.when(valid_s[k] == 1)
    def _():
        e = exp_s[k]
        gate = g_ref[pl.ds(e, 1), :]
        rank = jnp.where(gate > 0.0, rank_ref[pl.ds(e, 1), :], -1.0)
        rows = (base_s[k] + lax.broadcasted_iota(jnp.int32, (tm, tc), 0)).astype(F32)
        p = jnp.where(rows == rank, 1.0, 0.0)
        xs = jnp.dot(p.astype(BF16), u_ref[...], preferred_element_type=F32).astype(BF16)
        gsel = jnp.sum(p * gate, axis=1, keepdims=True)

        @pl.when(first_s[k] == 1)
        def _():
            x_ref[...] = xs
            gs_ref[...] = gsel

        @pl.when(first_s[k] == 0)
        def _():
            x_ref[...] = x_ref[...] + xs
            gs_ref[...] = gs_ref[...] + gsel


def _moe_gather(u, rank_all, gates, items, n_rows, ni_max, tc, tr):
    n, d = u.shape
    e = gates.shape[0]
    n_prefetch = len(items)
    spec = pltpu.PrefetchScalarGridSpec(
        num_scalar_prefetch=n_prefetch,
        grid=(ni_max,),
        in_specs=[pl.BlockSpec((tc, d), lambda k, t, c, *_: (c[k], 0)),
                  pl.BlockSpec((e, tc), lambda k, t, c, *_: (0, c[k])),
                  pl.BlockSpec((e, tc), lambda k, t, c, *_: (0, c[k])),
                  pl.BlockSpec(memory_space=pl.ANY),
                  pl.BlockSpec(memory_space=pl.ANY)],
        out_specs=(pl.BlockSpec((tr, d), lambda k, t, c, *_: (t[k], 0)),
                   pl.BlockSpec((tr, 1), lambda k, t, c, *_: (t[k], 0))),
    )
    return pl.pallas_call(
        _moe_gather_kernel,
        grid_spec=spec,
        out_shape=(jax.ShapeDtypeStruct((n_rows, d), BF16),
                   jax.ShapeDtypeStruct((n_rows, 1), F32)),
        input_output_aliases={n_prefetch + 3: 0, n_prefetch + 4: 1},
        compiler_params=_params(("arbitrary",)),
        name="moe_gather",
    )(*items, u, rank_all, gates, jnp.zeros((n_rows, d), BF16), jnp.zeros((n_rows, 1), F32))


def _moe_ffn_kernel(texp_s, tvalid_s, tsrc_s, x_ref, gs_ref, wg_ref, wu_ref, wd_ref, o_ref, acc_ref):
    t = pl.program_id(0)
    j = pl.program_id(1)
    valid = tvalid_s[t] == 1

    @pl.when(valid)
    def _():
        x = x_ref[...]
        g = jnp.dot(x, wg_ref[0], preferred_element_type=F32)
        up = jnp.dot(x, wu_ref[0], preferred_element_type=F32)
        hm = (_silu_mul(g, up) * gs_ref[...]).astype(BF16)
        y = jnp.dot(hm, wd_ref[0], preferred_element_type=F32)

        @pl.when(j == 0)
        def _():
            acc_ref[...] = y

        @pl.when(j > 0)
        def _():
            acc_ref[...] += y

    @pl.when(j == pl.num_programs(1) - 1)
    def _():
        @pl.when(valid)
        def _():
            o_ref[...] = acc_ref[...].astype(o_ref.dtype)

        @pl.when(jnp.logical_not(valid))
        def _():
            o_ref[...] = jnp.zeros(o_ref.shape, o_ref.dtype)


def _moe_ffn(xs, gs, tiles, wgu, wd, nt_max, tm, fc=FF_CHUNK):
    d = xs.shape[1]
    nj = D_FF // fc
    spec = pltpu.PrefetchScalarGridSpec(
        num_scalar_prefetch=3,
        grid=(nt_max, nj),
        in_specs=[pl.BlockSpec((tm, d), lambda t, j, te, tv, ts: (ts[t], 0)),
                  pl.BlockSpec((tm, 1), lambda t, j, te, tv, ts: (ts[t], 0)),
                  pl.BlockSpec((1, d, fc), lambda t, j, te, tv, ts: (te[t], 0, j)),
                  pl.BlockSpec((1, d, fc), lambda t, j, te, tv, ts: (te[t], 0, j + nj)),
                  pl.BlockSpec((1, fc, d), lambda t, j, te, tv, ts: (te[t], j, 0))],
        out_specs=pl.BlockSpec((tm, d), lambda t, j, te, tv, ts: (t, 0)),
        scratch_shapes=[pltpu.VMEM((tm, d), F32)],
    )
    return pl.pallas_call(
        _moe_ffn_kernel,
        grid_spec=spec,
        out_shape=jax.ShapeDtypeStruct((nt_max * tm, d), BF16),
        compiler_params=_params(("arbitrary", "arbitrary")),
        name="moe_ffn",
    )(*tiles, xs, gs, wgu, wgu, wd)


def _moe_combine_kernel(tile_s, chunk_s, exp_s, base_s, first_s, valid_s, h_ref, rcol_ref, ys_ref, o_ref):
    k = pl.program_id(0)
    tc, tm = h_ref.shape[0], ys_ref.shape[0]

    @pl.when(first_s[k] == 1)
    def _():
        o_ref[...] = h_ref[...]

    @pl.when(valid_s[k] == 1)
    def _():
        lane = lax.broadcasted_iota(jnp.int32, rcol_ref.shape, 1)
        rank = jnp.sum(jnp.where(lane == exp_s[k], rcol_ref[...], 0.0), axis=1, keepdims=True)
        cols = (base_s[k] + lax.broadcasted_iota(jnp.int32, (tc, tm), 1)).astype(F32)
        p = jnp.where(cols == rank, 1.0, 0.0).astype(BF16)
        o_ref[...] += jnp.dot(p, ys_ref[...], preferred_element_type=F32)


def _moe_combine(h, rank_col, ys, items, ni_max, tc, tm):
    n, d = h.shape
    e = rank_col.shape[1]
    spec = pltpu.PrefetchScalarGridSpec(
        num_scalar_prefetch=6,
        grid=(ni_max,),
        in_specs=[pl.BlockSpec((tc, d), lambda k, t, c, *_: (c[k], 0)),
                  pl.BlockSpec((tc, e), lambda k, t, c, *_: (c[k], 0)),
                  pl.BlockSpec((tm, d), lambda k, t, c, *_: (t[k], 0))],
        out_specs=pl.BlockSpec((tc, d), lambda k, t, c, *_: (c[k], 0)),
    )
    return pl.pallas_call(
        _moe_combine_kernel,
        grid_spec=spec,
        out_shape=jax.ShapeDtypeStruct((n, d), F32),
        compiler_params=_params(("arbitrary",)),
        name="moe_combine",
    )(*items, h, rank_col, ys)


def _moe_grouped(a, wo, x, nw, rwt, wgu, wd, tc=MOE_CHUNK, tm=MOE_TILE, tr=MOE_PERM_TILE):
    gates, u, h = _router(a, wo, x, nw, rwt)
    rank_all = _moe_rank(gates, tc)
    tiles, by_tile, by_chunk, nt_max, ni_max = _moe_plan(rank_all, gates, tc, tm, tr)
    xs, gs = _moe_gather(u, rank_all, gates, by_tile, nt_max * tm, ni_max, tc, tr)
    ys = _moe_ffn(xs, gs, tiles, wgu, wd, nt_max, tm)
    rank_col = jnp.where(gates > 0.0, rank_all, -1.0).T
    return _moe_combine(h, rank_col, ys, by_chunk, ni_max, tc, tr)


def kernel(x, a_norm_w, a_w_in, a_q_norm_w, a_k_norm_w, a_w_out, kv_norm_w, kv_w, b_norm_w, b_w_q, b_w_out,
           ffn_norm_w, dense_w_gate_up, dense_w_down, moe_router_w, moe_w_gate_up, moe_w_down):
    b, s, d = x.shape
    n = b * s
    topk = min(MAX_TOPK, s // 4)
    tq_a, ch_a, tq_b = DSA_TQ, DSA_CH, SB_TQ
    h = x.reshape(n, d)

    w_in = jnp.pad(a_w_in[0], ((0, 0), (0, A_IN_PAD - A_IN))).astype(BF16)
    q, k, vt, qit, ki, wt = _dsa_prep(h, a_norm_w[0], w_in, a_q_norm_w[0], a_k_norm_w[0], b, s, tq_a, ch_a)
    o = _dsa_attn(qit, wt, ki, q, k, vt, b, s, tq_a, ch_a, topk)
    h = _ffn_dense(o, a_w_out[0].astype(BF16), h, ffn_norm_w[0],
                   dense_w_gate_up[0].astype(BF16), dense_w_down[0].astype(BF16))

    k_sb, vt_sb, q_sb = _sb_proj(h, kv_norm_w, b_norm_w[0], kv_w.astype(BF16),
                                 (b_w_q[0] * (HEAD_DIM ** -0.5 * LOG2E)).astype(BF16), b, s, tq_b)
    o = _sb_attn(q_sb, k_sb, vt_sb, b, s, tq_b)
    h = _moe_grouped(o, b_w_out[0].astype(BF16), h, ffn_norm_w[1], moe_router_w[0].T,
                     moe_w_gate_up[0].astype(BF16), moe_w_down[0].astype(BF16))
    return h.reshape(b, s, d)
```
